```python
import jax, jax.numpy as jnp
from jax import lax
import numpy as np

D_MODEL = 2048
BATCH = 4
SEQ = 4096
DEPTH = 1
DEC_BATCH = 1
DEC_SEQ = 16384
PAST_LEN = 128

HEAD_DIM = 128
N_ATTN_HEADS = 8
N_KV_HEADS = 2
ATTN_WIDTH = N_ATTN_HEADS * HEAD_DIM
KV_WIDTH = N_KV_HEADS * HEAD_DIM
Q_BLOCK = 128
ROPE_THETA = 10000.0
ROPE_AXIS_DIM = HEAD_DIM // 2
GRID_W = 64
N_GLA_HEADS = 4
GLA_DK = 128
GLA_DV = 256
GLA_K_WIDTH = N_GLA_HEADS * GLA_DK
GLA_V_WIDTH = N_GLA_HEADS * GLA_DV
GATE_RANK = 16
GATE_TAU = 16.0
GLA_CHUNK = 64
MIX_WIDTH = ATTN_WIDTH + GLA_V_WIDTH
IN_SIZES = (ATTN_WIDTH, KV_WIDTH, KV_WIDTH, GLA_K_WIDTH, GLA_K_WIDTH, GLA_V_WIDTH, GLA_V_WIDTH, GATE_RANK, GATE_RANK)
D_IN = ATTN_WIDTH + 2 * KV_WIDTH + 2 * GLA_K_WIDTH + 2 * GLA_V_WIDTH + 2 * GATE_RANK
D_FF = -(-8 * D_MODEL // (3 * 256)) * 256
D_PLE = 256
EPS = 1e-6

kernel_name = "hymba_gqa_gla_bidir_encoder"


def rmsnorm(x, g):
    xf = x.astype(jnp.float32)
    y = xf * lax.rsqrt(jnp.mean(xf * xf, axis=-1, keepdims=True) + EPS)
    return (y * g.astype(jnp.float32)).astype(x.dtype)


def split_cols(z, sizes):
    out = []
    off = 0
    for s in sizes:
        out.append(z[..., off:off + s])
        off += s
    return out


def axial_rope_tables(seq_len):
    n_rows = seq_len // GRID_W
    row = jnp.repeat(jnp.arange(n_rows, dtype=jnp.float32), GRID_W)
    col = jnp.tile(jnp.arange(GRID_W, dtype=jnp.float32), n_rows)
    freqs = ROPE_THETA ** (-jnp.arange(0, ROPE_AXIS_DIM, 2, dtype=jnp.float32) / ROPE_AXIS_DIM)
    ang_r = row[:, None] * freqs[None, :]
    ang_c = col[:, None] * freqs[None, :]
    return (jnp.cos(ang_r)[:, None, :], jnp.sin(ang_r)[:, None, :],
            jnp.cos(ang_c)[:, None, :], jnp.sin(ang_c)[:, None, :])


def rotate_half(x, cos, sin):
    h = x.shape[-1] // 2
    x1, x2 = x[..., :h], x[..., h:]
    return jnp.concatenate([x1 * cos - x2 * sin, x2 * cos + x1 * sin], axis=-1)


def apply_axial_rope(x, rope):
    cos_r, sin_r, cos_c, sin_c = rope
    xf = x.astype(jnp.float32)
    xr = rotate_half(xf[..., :ROPE_AXIS_DIM], cos_r, sin_r)
    xc = rotate_half(xf[..., ROPE_AXIS_DIM:], cos_c, sin_c)
    return jnp.concatenate([xr, xc], axis=-1).astype(x.dtype)


def block_attention(q, k, v):
    B, S, _, D = q.shape
    nb = S // Q_BLOCK
    G = N_ATTN_HEADS // N_KV_HEADS
    qb = q.reshape(B, nb, Q_BLOCK, N_KV_HEADS, G, D).transpose(1, 0, 3, 4, 2, 5)
    kt = k.transpose(0, 2, 1, 3)
    vt = v.transpose(0, 2, 1, 3)
    scale = D ** -0.5

    def one_block(qblk):
        s = jnp.einsum('bhgqd,bhkd->bhgqk', qblk, kt).astype(jnp.float32) * scale
        p = jax.nn.softmax(s, axis=-1).astype(vt.dtype)
        return jnp.einsum('bhgqk,bhkd->bhgqd', p, vt)

    o = lax.map(one_block, qb)
    return o.transpose(1, 0, 4, 2, 3, 5).reshape(B, S, N_ATTN_HEADS * D)


def gla_chunked(q, k, v, log_a):
    B, H, S, DK = q.shape
    DV = v.shape[-1]
    C = GLA_CHUNK
    N = S // C
    q = q.reshape(B, H, N, C, DK)
    k = k.reshape(B, H, N, C, DK)
    v = v.reshape(B, H, N, C, DV)
    b = jnp.cumsum(log_a.reshape(B, H, N, C, DK), axis=3)
    b_last = b[:, :, :, -1:, :]
    q_dec = q * jnp.exp(b)
    k_dec = k * jnp.exp(-b)
    mask = jnp.tril(jnp.ones((C, C), dtype=bool))
    A = jnp.where(mask, jnp.einsum('bhnid,bhnjd->bhnij', q_dec, k_dec), 0.0)
    o_intra = jnp.einsum('bhnij,bhnjv->bhniv', A, v)
    chunk_kv = jnp.einsum('bhncd,bhncv->nbhdv', k * jnp.exp(b_last - b), v)
    chunk_decay = jnp.moveaxis(jnp.exp(b_last[:, :, :, 0, :]), 2, 0)

    def step(state, inp):
        kv, dec = inp
        return dec[..., None] * state + kv, state

    _, states = lax.scan(step, jnp.zeros((B, H, DK, DV), jnp.float32), (chunk_kv, chunk_decay))
    o_inter = jnp.einsum('bhncd,nbhdv->bhncv', q_dec, states)
    return (o_intra + o_inter).reshape(B, H, S, DV)


def to_heads(t, n_heads):
    B, S, W = t.shape
    return t.reshape(B, S, n_heads, W // n_heads).transpose(0, 2, 1, 3)


def gla_mixer(q, k, v, g_out, gf_low, gb_low, w_gf_up, b_gf, w_gb_up, b_gb, g_gla_norm):
    B, S, _ = q.shape
    f32 = jnp.float32
    qh = to_heads(q, N_GLA_HEADS).astype(f32) * (GLA_DK ** -0.5)
    kh = to_heads(k, N_GLA_HEADS).astype(f32)
    vh = to_heads(v, N_GLA_HEADS).astype(f32)
    la_f = to_heads(jax.nn.log_sigmoid((gf_low @ w_gf_up + b_gf).astype(f32)) / GATE_TAU, N_GLA_HEADS)
    la_b = to_heads(jax.nn.log_sigmoid((gb_low @ w_gb_up + b_gb).astype(f32)) / GATE_TAU, N_GLA_HEADS)
    o_f = gla_chunked(qh, kh, vh, la_f)
    flip = lambda t: jnp.flip(t, axis=2)
    o_b = flip(gla_chunked(flip(qh), flip(kh), flip(vh), flip(la_b)))
    diag = jnp.sum(qh * kh, axis=-1, keepdims=True) * vh
    o = (o_f + o_b - diag).transpose(0, 2, 1, 3)
    o = rmsnorm(o, g_gla_norm) * jax.nn.silu(g_out.astype(f32)).reshape(B, S, N_GLA_HEADS, GLA_DV)
    return o.reshape(B, S, GLA_V_WIDTH).astype(q.dtype)


def encoder_layer(x, p, rope, g_pre_mix, w_in, g_q, g_k, w_gf_up, b_gf, w_gb_up, b_gb, g_gla_norm,
                  w_out, g_post_mix, g_pre_ffn, w_gate_up, w_down, g_post_ffn,
                  g_ple_pre, w_ple_gate, w_ple_proj, g_ple_post):
    B, S, _ = x.shape
    h = rmsnorm(x, g_pre_mix)
    z = h @ w_in
    qa, ka, va, qg, kg, vg, og, gf_low, gb_low = split_cols(z, IN_SIZES)
    qa = apply_axial_rope(rmsnorm(qa.reshape(B, S, N_ATTN_HEADS, HEAD_DIM), g_q), rope)
    ka = apply_axial_rope(rmsnorm(ka.reshape(B, S, N_KV_HEADS, HEAD_DIM), g_k), rope)
    va = va.reshape(B, S, N_KV_HEADS, HEAD_DIM)
    attn_out = block_attention(qa, ka, va)
    gla_out = gla_mixer(qg, kg, vg, og, gf_low, gb_low, w_gf_up, b_gf, w_gb_up, b_gb, g_gla_norm)
    mix = jnp.concatenate([attn_out, gla_out], axis=-1) @ w_out
    x = x + rmsnorm(mix, g_post_mix)
    gate, up = jnp.split(rmsnorm(x, g_pre_ffn) @ w_gate_up, 2, axis=-1)
    x = x + rmsnorm((jax.nn.silu(gate) * up) @ w_down, g_post_ffn)
    pg = jax.nn.sigmoid(rmsnorm(x, g_ple_pre) @ w_ple_gate)
    x = x + rmsnorm((p @ w_ple_proj) * pg, g_ple_post)
    return x


def setup_inputs(seed: int = 0) -> dict:
    key = jax.random.key(seed)
    ks = iter(jax.random.split(key, 32))
    nrm = lambda shape: jax.random.normal(next(ks), shape, jnp.float32)
    gain = lambda n: 1.0 + 0.02 * nrm((DEPTH, n))
    return {
        "x_prompt": nrm((BATCH, SEQ, D_MODEL)),
        "x_sample": nrm((DEC_BATCH, DEC_SEQ, D_MODEL)),
        "p_prompt": nrm((DEPTH, BATCH, SEQ, D_PLE)),
        "p_sample": nrm((DEPTH, DEC_BATCH, DEC_SEQ, D_PLE)),
        "g_pre_mix": gain(D_MODEL),
        "w_in": nrm((DEPTH, D_MODEL, D_IN)) * D_MODEL ** -0.5,
        "g_q": gain(HEAD_DIM),
        "g_k": gain(HEAD_DIM),
        "w_gf_up": nrm((DEPTH, GATE_RANK, GLA_K_WIDTH)) * GATE_RANK ** -0.5,
        "b_gf": 1.0 + 0.1 * nrm((DEPTH, GLA_K_WIDTH)),
        "w_gb_up": nrm((DEPTH, GATE_RANK, GLA_K_WIDTH)) * GATE_RANK ** -0.5,
        "b_gb": 1.0 + 0.1 * nrm((DEPTH, GLA_K_WIDTH)),
        "g_gla_norm": gain(GLA_DV),
        "w_out": nrm((DEPTH, MIX_WIDTH, D_MODEL)) * MIX_WIDTH ** -0.5,
        "g_post_mix": gain(D_MODEL),
        "g_pre_ffn": gain(D_MODEL),
        "w_gate_up": nrm((DEPTH, D_MODEL, 2 * D_FF)) * D_MODEL ** -0.5,
        "w_down": nrm((DEPTH, D_FF, D_MODEL)) * D_FF ** -0.5,
        "g_post_ffn": gain(D_MODEL),
        "g_ple_pre": gain(D_MODEL),
        "w_ple_gate": nrm((DEPTH, D_MODEL, D_MODEL)) * D_MODEL ** -0.5,
        "w_ple_proj": nrm((DEPTH, D_PLE, D_MODEL)) * D_PLE ** -0.5,
        "g_ple_post": gain(D_MODEL),
    }


def reference(x_prompt, x_sample, p_prompt, p_sample, g_pre_mix, w_in, g_q, g_k, w_gf_up, b_gf,
              w_gb_up, b_gb, g_gla_norm, w_out, g_post_mix, g_pre_ffn, w_gate_up, w_down, g_post_ffn,
              g_ple_pre, w_ple_gate, w_ple_proj, g_ple_post):
    rope_prompt = axial_rope_tables(x_prompt.shape[1])
    rope_sample = axial_rope_tables(x_sample.shape[1])
    y_prompt = x_prompt
    y_sample = x_sample
    for i in range(DEPTH):
        w = (g_pre_mix[i], w_in[i], g_q[i], g_k[i], w_gf_up[i], b_gf[i], w_gb_up[i], b_gb[i],
             g_gla_norm[i], w_out[i], g_post_mix[i], g_pre_ffn[i], w_gate_up[i], w_down[i],
             g_post_ffn[i], g_ple_pre[i], w_ple_gate[i], w_ple_proj[i], g_ple_post[i])
        y_prompt = encoder_layer(y_prompt, p_prompt[i], rope_prompt, *w)
        y_sample = encoder_layer(y_sample, p_sample[i], rope_sample, *w)
    return (y_prompt, y_sample)
```

```python
import functools

import jax
import jax.numpy as jnp
from jax import lax
from jax.experimental import pallas as pl
from jax.experimental.pallas import tpu as pltpu

D_MODEL = 2048
HEAD_DIM = 128
N_ATTN_HEADS = 8
N_KV_HEADS = 2
KV_GROUP = N_ATTN_HEADS // N_KV_HEADS
ATTN_WIDTH = N_ATTN_HEADS * HEAD_DIM
KV_WIDTH = N_KV_HEADS * HEAD_DIM
ROPE_THETA = 10000.0
ROPE_AXIS_DIM = HEAD_DIM // 2
GRID_W = 64
N_GLA_HEADS = 4
GLA_DK = 128
GLA_DV = 256
GLA_K_WIDTH = N_GLA_HEADS * GLA_DK
GLA_V_WIDTH = N_GLA_HEADS * GLA_DV
GATE_RANK = 16
GATE_TAU = 16.0
GLA_CHUNK = 64
MIX_WIDTH = ATTN_WIDTH + GLA_V_WIDTH
D_FF = 5632
D_PLE = 256
EPS = 1e-6

_OFF_QA = 0
_OFF_KA = _OFF_QA + ATTN_WIDTH
_OFF_VA = _OFF_KA + KV_WIDTH
_OFF_QG = _OFF_VA + KV_WIDTH
_OFF_KG = _OFF_QG + GLA_K_WIDTH
_OFF_VG = _OFF_KG + GLA_K_WIDTH
_OFF_OG = _OFF_VG + GLA_V_WIDTH
_OFF_LOW = _OFF_OG + GLA_V_WIDTH
D_MAIN = _OFF_LOW

V7X_VMEM_LIMIT_BYTES = 56 * 1024 * 1024

F32 = jnp.float32
BF16 = jnp.bfloat16


def _rms(xf, g):
    ms = jnp.mean(xf * xf, axis=-1, keepdims=True)
    return xf * lax.rsqrt(ms + EPS) * g


def _dot(a, b):
    return jnp.dot(a, b, preferred_element_type=F32)


def _dot_nt(a, b):
    return lax.dot_general(a, b, (((1,), (1,)), ((), ())), preferred_element_type=F32)


def _dot_tn(a, b):
    return lax.dot_general(a, b, (((0,), (0,)), ((), ())), preferred_element_type=F32)


def _sigmoid(x):
    return 1.0 / (1.0 + jnp.exp(-x))


def _rope(z, cos, sin_signed, first_half):
    swapped = jnp.where(first_half, pltpu.roll(z, HEAD_DIM - 32, axis=1), pltpu.roll(z, 32, axis=1))
    return z * cos + swapped * sin_signed


def _inproj_kernel(x_ref, gpre_ref, w_ref, wlow_ref, gq_ref, gk_ref, cos_ref, sin_ref, wup_ref, bup_ref,
                   qa_ref, ka_ref, va_ref, qg_ref, kg_ref, vg_ref, og_ref, laf_ref, lab_ref):
    h = _rms(x_ref[...], gpre_ref[...]).astype(BF16)
    cos = cos_ref[...]
    sin = sin_ref[...]
    lane = lax.broadcasted_iota(jnp.int32, cos.shape, 1)
    first_half = (lane % ROPE_AXIS_DIM) < (ROPE_AXIS_DIM // 2)

    def qk_head(z, g, scale):
        return (_rope(_rms(z, g), cos, sin, first_half) * scale).astype(BF16)

    slab = 4 * HEAD_DIM
    for s in range(ATTN_WIDTH // slab):
        z = _dot(h, w_ref[:, _OFF_QA + s * slab:_OFF_QA + (s + 1) * slab])
        for j in range(slab // HEAD_DIM):
            c = s * slab + j * HEAD_DIM
            qa_ref[:, c:c + HEAD_DIM] = qk_head(z[:, j * HEAD_DIM:(j + 1) * HEAD_DIM], gq_ref[...], HEAD_DIM ** -0.5)

    z = _dot(h, w_ref[:, _OFF_KA:_OFF_KA + 2 * KV_WIDTH])
    for j in range(N_KV_HEADS):
        ka_ref[:, j * HEAD_DIM:(j + 1) * HEAD_DIM] = qk_head(z[:, j * HEAD_DIM:(j + 1) * HEAD_DIM], gk_ref[...], 1.0)
    va_ref[...] = z[:, KV_WIDTH:].astype(BF16)

    qg_ref[...] = (_dot(h, w_ref[:, _OFF_QG:_OFF_QG + GLA_K_WIDTH]) * (GLA_DK ** -0.5)).astype(BF16)
    kg_ref[...] = _dot(h, w_ref[:, _OFF_KG:_OFF_KG + GLA_K_WIDTH]).astype(BF16)
    half = GLA_V_WIDTH // 2
    for s in range(2):
        vg_ref[:, s * half:(s + 1) * half] = _dot(h, w_ref[:, _OFF_VG + s * half:_OFF_VG + (s + 1) * half]).astype(BF16)
    for s in range(2):
        og = _dot(h, w_ref[:, _OFF_OG + s * half:_OFF_OG + (s + 1) * half])
        og_ref[:, s * half:(s + 1) * half] = (og * _sigmoid(og)).astype(BF16)

    low = _dot(h, wlow_ref[...]).astype(BF16)
    pre = _dot(low, wup_ref[...]) + bup_ref[...]
    log_a = (jnp.minimum(pre, 0.0) - jnp.log(1.0 + jnp.exp(-jnp.abs(pre)))) * (1.0 / GATE_TAU)
    laf_ref[...] = log_a[:, :GLA_K_WIDTH]
    lab_ref[...] = log_a[:, GLA_K_WIDTH:]


def _in_proj(x, g_pre, w_main, w_low, g_q, g_k, cos, sin, w_up, b_up, seq_len, tm):
    n_tok = x.shape[0]
    n_pos_blocks = seq_len // tm
    row = lambda w: pl.BlockSpec((tm, w), lambda i: (i, 0))
    whole = lambda a: pl.BlockSpec(a.shape, lambda i: (0,) * a.ndim)
    pos = pl.BlockSpec((tm, HEAD_DIM), lambda i: (i % n_pos_blocks, 0))
    out_widths = (ATTN_WIDTH, KV_WIDTH, KV_WIDTH, GLA_K_WIDTH, GLA_K_WIDTH, GLA_V_WIDTH, GLA_V_WIDTH)
    out_shape = [jax.ShapeDtypeStruct((n_tok, w), BF16) for w in out_widths]
    out_shape += [jax.ShapeDtypeStruct((n_tok, GLA_K_WIDTH), F32)] * 2
    out_specs = [row(w) for w in out_widths] + [row(GLA_K_WIDTH)] * 2
    return pl.pallas_call(
        _inproj_kernel,
        grid=(n_tok // tm,),
        in_specs=[row(D_MODEL), whole(g_pre), whole(w_main), whole(w_low), whole(g_q), whole(g_k), pos, pos,
                  whole(w_up), whole(b_up)],
        out_specs=out_specs,
        out_shape=out_shape,
        compiler_params=pltpu.CompilerParams(dimension_semantics=("parallel",),
                                             vmem_limit_bytes=V7X_VMEM_LIMIT_BYTES),
        name="in_proj",
    )(x, g_pre, w_main, w_low, g_q, g_k, cos, sin, w_up, b_up)


def _attn_kernel(q_ref, k_ref, v_ref, o_ref, *, kc):
    tq = q_ref.shape[0]
    n_chunks = k_ref.shape[0] // kc
    q = q_ref[...]

    def body(c, carry):
        m, l, acc = carry
        start = pl.multiple_of(c * kc, kc)
        k = k_ref[pl.ds(start, kc), :]
        v = v_ref[pl.ds(start, kc), :]
        s = _dot_nt(q, k)
        m_new = jnp.maximum(m, jnp.max(s, axis=-1, keepdims=True))
        alpha = jnp.exp(m - m_new)
        p = jnp.exp(s - m_new)
        l_new = alpha * l + jnp.sum(p, axis=-1, keepdims=True)
        acc_new = alpha * acc + _dot(p.astype(BF16), v)
        return m_new, l_new, acc_new

    init = (jnp.full((tq, 1), -jnp.inf, F32), jnp.zeros((tq, 1), F32), jnp.zeros((tq, HEAD_DIM), F32))
    _, l, acc = lax.fori_loop(0, n_chunks, body, init)
    o_ref[...] = (acc / l).astype(o_ref.dtype)


def _attention(q, k, v, batch, seq_len, tq, kc):
    n_q = seq_len // tq
    q_spec = pl.BlockSpec((tq, HEAD_DIM), lambda b, h, g, i: (b * n_q + i, h * KV_GROUP + g))
    kv_spec = pl.BlockSpec((seq_len, HEAD_DIM), lambda b, h, g, i: (b, h))
    return pl.pallas_call(
        functools.partial(_attn_kernel, kc=kc),
        grid=(batch, N_KV_HEADS, KV_GROUP, n_q),
        in_specs=[q_spec, kv_spec, kv_spec],
        out_specs=q_spec,
        out_shape=jax.ShapeDtypeStruct(q.shape, BF16),
        compiler_params=pltpu.CompilerParams(
            dimension_semantics=("parallel", "parallel", "parallel", "parallel"),
            vmem_limit_bytes=V7X_VMEM_LIMIT_BYTES),
        name="attention",
    )(q, k, v)


def _gla_scan(q_ref, k_ref, v_ref, la_ref, o_ref, state_ref, *, reverse):
    n = GLA_CHUNK
    row = lax.broadcasted_iota(jnp.int32, (n, n), 0)
    col = lax.broadcasted_iota(jnp.int32, (n, n), 1)
    if reverse:
        cum = (col >= row).astype(BF16)
        keep = col > row
        edge = 0
    else:
        cum = (col <= row).astype(BF16)
        keep = col <= row
        edge = n - 1
    n_chunks = q_ref.shape[0] // n
    order = range(n_chunks - 1, -1, -1) if reverse else range(n_chunks)
    for c in order:
        rows = slice(c * n, (c + 1) * n)
        la = la_ref[rows, :]
        la_hi = la.astype(BF16)
        la_lo = (la - la_hi.astype(F32)).astype(BF16)
        b = _dot(cum, la_hi) + _dot(cum, la_lo)
        b_edge = b[edge:edge + 1, :]
        q = q_ref[rows, :].astype(F32)
        k = k_ref[rows, :].astype(F32)
        v = v_ref[rows, :]
        q_dec = (q * jnp.exp(b)).astype(BF16)
        k_dec = (k * jnp.exp(-b)).astype(BF16)
        k_out = (k * jnp.exp(b_edge - b)).astype(BF16)
        a = jnp.where(keep, _dot_nt(q_dec, k_dec), 0.0).astype(BF16)
        state = state_ref[...]
        o_ref[rows, :] = _dot(a, v) + _dot_nt(q_dec, state.astype(BF16))
        state_ref[...] = state * jnp.exp(b_edge) + _dot_tn(v, k_out)


def _gla_kernel(qf_ref, kf_ref, vf_ref, laf_ref, qb_ref, kb_ref, vb_ref, lab_ref, of_ref, ob_ref, sf_ref, sb_ref):
    @pl.when(pl.program_id(2) == 0)
    def _():
        sf_ref[...] = jnp.zeros_like(sf_ref)
        sb_ref[...] = jnp.zeros_like(sb_ref)

    _gla_scan(qf_ref, kf_ref, vf_ref, laf_ref, of_ref, sf_ref, reverse=False)
    _gla_scan(qb_ref, kb_ref, vb_ref, lab_ref, ob_ref, sb_ref, reverse=True)


def _gla(q, k, v, la_f, la_b, batch, seq_len, tl):
    n_blk = seq_len // tl
    fwd = lambda w: pl.BlockSpec((tl, w), lambda b, h, i: (b * n_blk + i, h))
    bwd = lambda w: pl.BlockSpec((tl, w), lambda b, h, i: (b * n_blk + n_blk - 1 - i, h))
    out = jax.ShapeDtypeStruct((q.shape[0], GLA_V_WIDTH), F32)
    return pl.pallas_call(
        _gla_kernel,
        grid=(batch, N_GLA_HEADS, n_blk),
        in_specs=[fwd(GLA_DK), fwd(GLA_DK), fwd(GLA_DV), fwd(GLA_DK),
                  bwd(GLA_DK), bwd(GLA_DK), bwd(GLA_DV), bwd(GLA_DK)],
        out_specs=[fwd(GLA_DV), bwd(GLA_DV)],
        out_shape=[out, out],
        scratch_shapes=[pltpu.VMEM((GLA_DV, GLA_DK), F32), pltpu.VMEM((GLA_DV, GLA_DK), F32)],
        compiler_params=pltpu.CompilerParams(dimension_semantics=("parallel", "parallel", "arbitrary"),
                                             vmem_limit_bytes=V7X_VMEM_LIMIT_BYTES),
        name="gla",
    )(q, k, v, la_f, q, k, v, la_b)


def _outproj_kernel(attn_ref, of_ref, ob_ref, og_ref, x_ref, w_ref, ggla_ref, gpost_ref, y_ref):
    mix = _dot(attn_ref[...], w_ref[:ATTN_WIDTH, :])
    for hd in range(N_GLA_HEADS):
        cols = slice(hd * GLA_DV, (hd + 1) * GLA_DV)
        o = _rms(of_ref[:, cols] + ob_ref[:, cols], ggla_ref[...]) * og_ref[:, cols].astype(F32)
        mix += _dot(o.astype(BF16), w_ref[ATTN_WIDTH + hd * GLA_DV:ATTN_WIDTH + (hd + 1) * GLA_DV, :])
    y_ref[...] = x_ref[...] + _rms(mix, gpost_ref[...])


def _out_proj(attn, o_f, o_b, og, x, w_out, g_gla, g_post, tm):
    n_tok = x.shape[0]
    row = lambda w: pl.BlockSpec((tm, w), lambda i: (i, 0))
    whole = lambda a: pl.BlockSpec(a.shape, lambda i: (0,) * a.ndim)
    return pl.pallas_call(
        _outproj_kernel,
        grid=(n_tok // tm,),
        in_specs=[row(ATTN_WIDTH), row(GLA_V_WIDTH), row(GLA_V_WIDTH), row(GLA_V_WIDTH), row(D_MODEL),
                  whole(w_out), whole(g_gla), whole(g_post)],
        out_specs=row(D_MODEL),
        out_shape=jax.ShapeDtypeStruct(x.shape, F32),
        compiler_params=pltpu.CompilerParams(dimension_semantics=("parallel",),
                                             vmem_limit_bytes=V7X_VMEM_LIMIT_BYTES),
        name="out_proj",
    )(attn, o_f, o_b, og, x, w_out, g_gla, g_post)


def _ffn_kernel(x_ref, gpre_ref, wg_ref, wu_ref, wd_ref, gpost_ref, y_ref, h_ref, acc_ref):
    f = pl.program_id(1)

    @pl.when(f == 0)
    def _():
        h_ref[...] = _rms(x_ref[...], gpre_ref[...]).astype(BF16)

    h = h_ref[...]
    gate = _dot(h, wg_ref[...])
    up = _dot(h, wu_ref[...])
    part = _dot((gate * _sigmoid(gate) * up).astype(BF16), wd_ref[...])

    @pl.when(f == 0)
    def _():
        acc_ref[...] = part

    @pl.when(f > 0)
    def _():
        acc_ref[...] += part

    @pl.when(f == pl.num_programs(1) - 1)
    def _():
        y_ref[...] = x_ref[...] + _rms(acc_ref[...], gpost_ref[...])


def _ffn(x, g_pre, w_gate_up, w_down, g_post, tm, tf):
    n_tok = x.shape[0]
    n_f = D_FF // tf
    row = pl.BlockSpec((tm, D_MODEL), lambda i, f: (i, 0))
    vec = lambda a: pl.BlockSpec(a.shape, lambda i, f: (0,) * a.ndim)
    return pl.pallas_call(
        _ffn_kernel,
        grid=(n_tok // tm, n_f),
        in_specs=[row, vec(g_pre),
                  pl.BlockSpec((D_MODEL, tf), lambda i, f: (0, f)),
                  pl.BlockSpec((D_MODEL, tf), lambda i, f: (0, f + n_f)),
                  pl.BlockSpec((tf, D_MODEL), lambda i, f: (f, 0)),
                  vec(g_post)],
        out_specs=row,
        out_shape=jax.ShapeDtypeStruct(x.shape, F32),
        scratch_shapes=[pltpu.VMEM((tm, D_MODEL), BF16), pltpu.VMEM((tm, D_MODEL), F32)],
        compiler_params=pltpu.CompilerParams(dimension_semantics=("parallel", "arbitrary"),
                                             vmem_limit_bytes=V7X_VMEM_LIMIT_BYTES),
        name="ffn",
    )(x, g_pre, w_gate_up, w_gate_up, w_down, g_post)


def _ple_kernel(x_ref, p_ref, gpre_ref, wg_ref, wp_ref, gpost_ref, y_ref):
    x = x_ref[...]
    gate = _sigmoid(_dot(_rms(x, gpre_ref[...]).astype(BF16), wg_ref[...]))
    emb = _dot(p_ref[...].astype(BF16), wp_ref[...])
    y_ref[...] = x + _rms(emb * gate, gpost_ref[...])


def _ple(x, p, g_pre, w_gate, w_proj, g_post, tm):
    n_tok = x.shape[0]
    row = lambda w: pl.BlockSpec((tm, w), lambda i: (i, 0))
    whole = lambda a: pl.BlockSpec(a.shape, lambda i: (0,) * a.ndim)
    return pl.pallas_call(
        _ple_kernel,
        grid=(n_tok // tm,),
        in_specs=[row(D_MODEL), row(D_PLE), whole(g_pre), whole(w_gate), whole(w_proj), whole(g_post)],
        out_specs=row(D_MODEL),
        out_shape=jax.ShapeDtypeStruct(x.shape, F32),
        compiler_params=pltpu.CompilerParams(dimension_semantics=("parallel",),
                                             vmem_limit_bytes=V7X_VMEM_LIMIT_BYTES),
        name="ple",
    )(x, p, g_pre, w_gate, w_proj, g_post)


def _rope_tables(seq_len):
    t = jnp.arange(seq_len, dtype=jnp.int32)
    row = (t // GRID_W).astype(F32)
    col = (t % GRID_W).astype(F32)
    freqs = ROPE_THETA ** (-jnp.arange(0, ROPE_AXIS_DIM, 2, dtype=F32) / ROPE_AXIS_DIM)
    ang_r = row[:, None] * freqs[None, :]
    ang_c = col[:, None] * freqs[None, :]
    cos = jnp.concatenate([jnp.cos(ang_r), jnp.cos(ang_r), jnp.cos(ang_c), jnp.cos(ang_c)], axis=-1)
    sin = jnp.concatenate([-jnp.sin(ang_r), jnp.sin(ang_r), -jnp.sin(ang_c), jnp.sin(ang_c)], axis=-1)
    return cos, sin


def _layer(x3, p3, w, *, tm_in, tq, kc, tl, tm_out, tm_ffn, tf, tm_ple):
    batch, seq_len, _ = x3.shape
    x = x3.reshape(batch * seq_len, D_MODEL)
    p = p3.reshape(batch * seq_len, D_PLE)
    cos, sin = _rope_tables(seq_len)
    qa, ka, va, qg, kg, vg, og, la_f, la_b = _in_proj(
        x, w["g_pre_mix"], w["w_main"], w["w_low"], w["g_q"], w["g_k"], cos, sin, w["w_up"], w["b_up"],
        seq_len, tm_in)
    attn = _attention(qa, ka, va, batch, seq_len, tq, kc)
    o_f, o_b = _gla(qg, kg, vg, la_f, la_b, batch, seq_len, tl)
    x = _out_proj(attn, o_f, o_b, og, x, w["w_out"], w["g_gla_norm"], w["g_post_mix"], tm_out)
    x = _ffn(x, w["g_pre_ffn"], w["w_gate_up"], w["w_down"], w["g_post_ffn"], tm_ffn, tf)
    x = _ple(x, p, w["g_ple_pre"], w["w_ple_gate"], w["w_ple_proj"], w["g_ple_post"], tm_ple)
    return x.reshape(batch, seq_len, D_MODEL)


def _layer_weights(i, g_pre_mix, w_in, g_q, g_k, w_gf_up, b_gf, w_gb_up, b_gb, g_gla_norm, w_out, g_post_mix,
                   g_pre_ffn, w_gate_up, w_down, g_post_ffn, g_ple_pre, w_ple_gate, w_ple_proj, g_ple_post):
    vec = lambda g: g[i].astype(F32)[None, :]
    zeros = jnp.zeros((GATE_RANK, GLA_K_WIDTH), F32)
    w_up = jnp.concatenate([jnp.concatenate([w_gf_up[i], zeros], axis=1),
                            jnp.concatenate([zeros, w_gb_up[i]], axis=1)], axis=0)
    return {
        "g_pre_mix": vec(g_pre_mix), "w_main": w_in[i, :, :D_MAIN].astype(BF16),
        "w_low": w_in[i, :, D_MAIN:].astype(BF16), "g_q": vec(g_q), "g_k": vec(g_k),
        "w_up": w_up.astype(BF16), "b_up": jnp.concatenate([b_gf[i], b_gb[i]]).astype(F32)[None, :],
        "g_gla_norm": vec(g_gla_norm), "w_out": w_out[i].astype(BF16), "g_post_mix": vec(g_post_mix),
        "g_pre_ffn": vec(g_pre_ffn), "w_gate_up": w_gate_up[i].astype(BF16), "w_down": w_down[i].astype(BF16),
        "g_post_ffn": vec(g_post_ffn), "g_ple_pre": vec(g_ple_pre), "w_ple_gate": w_ple_gate[i].astype(BF16),
        "w_ple_proj": w_ple_proj[i].astype(BF16), "g_ple_post": vec(g_ple_post),
    }


_TILES = dict(tm_in=256, tq=512, kc=512, tl=256, tm_out=512, tm_ffn=512, tf=512, tm_ple=512)


def kernel(x_prompt, x_sample, p_prompt, p_sample, g_pre_mix, w_in, g_q, g_k, w_gf_up, b_gf, w_gb_up, b_gb,
           g_gla_norm, w_out, g_post_mix, g_pre_ffn, w_gate_up, w_down, g_post_ffn, g_ple_pre, w_ple_gate,
           w_ple_proj, g_ple_post):
    y_prompt = x_prompt
    y_sample = x_sample
    for i in range(g_pre_mix.shape[0]):
        w = _layer_weights(i, g_pre_mix, w_in, g_q, g_k, w_gf_up, b_gf, w_gb_up, b_gb, g_gla_norm, w_out,
                           g_post_mix, g_pre_ffn, w_gate_up, w_down, g_post_ffn, g_ple_pre, w_ple_gate,
                           w_ple_proj, g_ple_post)
        y_prompt = _layer(y_prompt, p_prompt[i], w, **_TILES)
        y_sample = _layer(y_sample, p_sample[i], w, **_TILES)
    return (y_prompt, y_sample)
```

```python
import functools

import jax
import jax.numpy as jnp
from jax import lax
from jax.experimental import pallas as pl
from jax.experimental.pallas import tpu as pltpu

D_MODEL = 2048
HEAD_DIM = 128
N_ATTN_HEADS = 8
N_KV_HEADS = 2
KV_GROUP = N_ATTN_HEADS // N_KV_HEADS
ATTN_WIDTH = N_ATTN_HEADS * HEAD_DIM
KV_WIDTH = N_KV_HEADS * HEAD_DIM
ROPE_THETA = 10000.0
ROPE_AXIS_DIM = HEAD_DIM // 2
GRID_W = 64
N_GLA_HEADS = 4
GLA_DK = 128
GLA_DV = 256
GLA_K_WIDTH = N_GLA_HEADS * GLA_DK
GLA_V_WIDTH = N_GLA_HEADS * GLA_DV
GATE_RANK = 16
GATE_TAU = 16.0
GLA_CHUNK = 64
MIX_WIDTH = ATTN_WIDTH + GLA_V_WIDTH
D_FF = 5632
D_PLE = 256
EPS = 1e-6

_OFF_QA = 0
_OFF_KA = _OFF_QA + ATTN_WIDTH
_OFF_VA = _OFF_KA + KV_WIDTH
_OFF_QG = _OFF_VA + KV_WIDTH
_OFF_KG = _OFF_QG + GLA_K_WIDTH
_OFF_VG = _OFF_KG + GLA_K_WIDTH
_OFF_OG = _OFF_VG + GLA_V_WIDTH
_OFF_LOW = _OFF_OG + GLA_V_WIDTH
D_MAIN = _OFF_LOW

V7X_VMEM_LIMIT_BYTES = 56 * 1024 * 1024

F32 = jnp.float32
BF16 = jnp.bfloat16

_Q_SCALE = HEAD_DIM ** -0.5 * 1.4426950408889634


def _rms(xf, g):
    ms = jnp.mean(xf * xf, axis=-1, keepdims=True)
    return xf * lax.rsqrt(ms + EPS) * g


def _dot(a, b):
    return jnp.dot(a, b, preferred_element_type=F32)


def _dot_nt(a, b):
    return lax.dot_general(a, b, (((1,), (1,)), ((), ())), preferred_element_type=F32)


def _dot_tn(a, b):
    return lax.dot_general(a, b, (((0,), (0,)), ((), ())), preferred_element_type=F32)


def _sigmoid(x):
    return 1.0 / (1.0 + jnp.exp(-x))


def _rope(z, cos, sin_signed, first_half):
    swapped = jnp.where(first_half, pltpu.roll(z, HEAD_DIM - 32, axis=1), pltpu.roll(z, 32, axis=1))
    return z * cos + swapped * sin_signed


def _inproj_kernel(x_ref, gpre_ref, w_ref, wlow_ref, gq_ref, gk_ref, cos_ref, sin_ref, wup_ref, bup_ref,
                   qa_ref, ka_ref, va_ref, qg_ref, kg_ref, vg_ref, og_ref, laf_ref, lab_ref):
    h = _rms(x_ref[...], gpre_ref[...]).astype(BF16)
    cos = cos_ref[...]
    sin = sin_ref[...]
    lane = lax.broadcasted_iota(jnp.int32, cos.shape, 1)
    first_half = (lane % ROPE_AXIS_DIM) < (ROPE_AXIS_DIM // 2)

    def qk_head(z, g, scale):
        return (_rope(_rms(z, g), cos, sin, first_half) * scale).astype(BF16)

    slab = 4 * HEAD_DIM
    for s in range(ATTN_WIDTH // slab):
        z = _dot(h, w_ref[:, _OFF_QA + s * slab:_OFF_QA + (s + 1) * slab])
        for j in range(slab // HEAD_DIM):
            c = s * slab + j * HEAD_DIM
            qa_ref[:, c:c + HEAD_DIM] = qk_head(z[:, j * HEAD_DIM:(j + 1) * HEAD_DIM], gq_ref[...], _Q_SCALE)

    z = _dot(h, w_ref[:, _OFF_KA:_OFF_KA + 2 * KV_WIDTH])
    for j in range(N_KV_HEADS):
        ka_ref[:, j * HEAD_DIM:(j + 1) * HEAD_DIM] = qk_head(z[:, j * HEAD_DIM:(j + 1) * HEAD_DIM], gk_ref[...], 1.0)
    va_ref[...] = z[:, KV_WIDTH:].astype(BF16)

    qg_ref[...] = (_dot(h, w_ref[:, _OFF_QG:_OFF_QG + GLA_K_WIDTH]) * (GLA_DK ** -0.5)).astype(BF16)
    kg_ref[...] = _dot(h, w_ref[:, _OFF_KG:_OFF_KG + GLA_K_WIDTH]).astype(BF16)
    half = GLA_V_WIDTH // 2
    for s in range(2):
        vg_ref[:, s * half:(s + 1) * half] = _dot(h, w_ref[:, _OFF_VG + s * half:_OFF_VG + (s + 1) * half]).astype(BF16)
    for s in range(2):
        og = _dot(h, w_ref[:, _OFF_OG + s * half:_OFF_OG + (s + 1) * half])
        og_ref[:, s * half:(s + 1) * half] = (og * _sigmoid(og)).astype(BF16)

    low = _dot(h, wlow_ref[...]).astype(BF16)
    pre = _dot(low, wup_ref[...]) + bup_ref[...]
    log_a = (jnp.minimum(pre, 0.0) - jnp.log(1.0 + jnp.exp(-jnp.abs(pre)))) * (1.0 / GATE_TAU)
    laf_ref[...] = log_a[:, :GLA_K_WIDTH]
    lab_ref[...] = log_a[:, GLA_K_WIDTH:]


def _in_proj(x, g_pre, w_main, w_low, g_q, g_k, cos, sin, w_up, b_up, seq_len, tm):
    n_tok = x.shape[0]
    n_pos_blocks = seq_len // tm
    row = lambda w: pl.BlockSpec((tm, w), lambda i: (i, 0))
    whole = lambda a: pl.BlockSpec(a.shape, lambda i: (0,) * a.ndim)
    pos = pl.BlockSpec((tm, HEAD_DIM), lambda i: (i % n_pos_blocks, 0))
    out_widths = (ATTN_WIDTH, KV_WIDTH, KV_WIDTH, GLA_K_WIDTH, GLA_K_WIDTH, GLA_V_WIDTH, GLA_V_WIDTH)
    out_shape = [jax.ShapeDtypeStruct((n_tok, w), BF16) for w in out_widths]
    out_shape += [jax.ShapeDtypeStruct((n_tok, GLA_K_WIDTH), F32)] * 2
    out_specs = [row(w) for w in out_widths] + [row(GLA_K_WIDTH)] * 2
    return pl.pallas_call(
        _inproj_kernel,
        grid=(n_tok // tm,),
        in_specs=[row(D_MODEL), whole(g_pre), whole(w_main), whole(w_low), whole(g_q), whole(g_k), pos, pos,
                  whole(w_up), whole(b_up)],
        out_specs=out_specs,
        out_shape=out_shape,
        compiler_params=pltpu.CompilerParams(dimension_semantics=("parallel",),
                                             vmem_limit_bytes=V7X_VMEM_LIMIT_BYTES),
        name="in_proj",
    )(x, g_pre, w_main, w_low, g_q, g_k, cos, sin, w_up, b_up)


def _col_reduce(x, pair_op, final_reduce):
    while x.shape[0] > 8:
        half = x.shape[0] // 2
        x = pair_op(x[:half], x[half:])
    return final_reduce(x, axis=0, keepdims=True)


def _attn_kernel(q_ref, k_ref, vt_ref, o_ref, st_even_ref, st_odd_ref, acc_ref):
    tq = q_ref.shape[0]
    n_chunks, _, kc = vt_ref.shape
    assert n_chunks % 2 == 0
    acc_ref[...] = jnp.zeros_like(acc_ref)

    def scores(c, st_ref, m_all):
        k = k_ref[pl.ds(pl.multiple_of(c * kc, kc), kc), :]
        m_out = []
        for g in range(KV_GROUP):
            st = _dot_nt(k, q_ref[:, g * HEAD_DIM:(g + 1) * HEAD_DIM])
            st_ref[g] = st
            m_out.append(jnp.maximum(m_all[g], _col_reduce(st, jnp.maximum, jnp.max)))
        return tuple(m_out)

    def accumulate(c, st_ref, m_old, m_new, l_all):
        vt = vt_ref[c]
        l_out = []
        for g in range(KV_GROUP):
            alpha = jnp.exp2(m_old[g] - m_new[g])
            p = jnp.exp2(st_ref[g] - m_new[g])
            l_out.append(alpha * l_all[g] + _col_reduce(p, jnp.add, jnp.sum))
            acc_ref[g] = alpha * acc_ref[g] + _dot(vt, p.astype(BF16))
        return tuple(l_out)

    def pair(i, carry):
        m_old, m_cur, l_all = carry
        c = 2 * i
        m_odd = scores(c + 1, st_odd_ref, m_cur)
        l_all = accumulate(c, st_even_ref, m_old, m_cur, l_all)
        m_even = scores(c + 2, st_even_ref, m_odd)
        l_all = accumulate(c + 1, st_odd_ref, m_cur, m_odd, l_all)
        return m_odd, m_even, l_all

    m_init = tuple(jnp.full((1, tq), -jnp.inf, F32) for _ in range(KV_GROUP))
    l_init = tuple(jnp.zeros((1, tq), F32) for _ in range(KV_GROUP))
    m_old, m_cur, l_all = lax.fori_loop(0, n_chunks // 2 - 1, pair,
                                        (m_init, scores(0, st_even_ref, m_init), l_init))
    m_last = scores(n_chunks - 1, st_odd_ref, m_cur)
    l_all = accumulate(n_chunks - 2, st_even_ref, m_old, m_cur, l_all)
    l_all = accumulate(n_chunks - 1, st_odd_ref, m_cur, m_last, l_all)
    for g in range(KV_GROUP):
        o_ref[:, g * HEAD_DIM:(g + 1) * HEAD_DIM] = (acc_ref[g] / l_all[g]).T.astype(o_ref.dtype)


def _attention(q, k, v, batch, seq_len, tq, kc):
    n_q = seq_len // tq
    n_chunks = seq_len // kc
    vt = v.reshape(batch, n_chunks, kc, N_KV_HEADS, HEAD_DIM).transpose(0, 3, 1, 4, 2)
    vt = vt.reshape(batch * N_KV_HEADS, n_chunks, HEAD_DIM, kc)
    q_spec = pl.BlockSpec((tq, KV_GROUP * HEAD_DIM), lambda b, h, i: (b * n_q + i, h))
    k_spec = pl.BlockSpec((seq_len, HEAD_DIM), lambda b, h, i: (b, h))
    vt_spec = pl.BlockSpec((None, n_chunks, HEAD_DIM, kc), lambda b, h, i: (b * N_KV_HEADS + h, 0, 0, 0))
    return pl.pallas_call(
        _attn_kernel,
        grid=(batch, N_KV_HEADS, n_q),
        in_specs=[q_spec, k_spec, vt_spec],
        out_specs=q_spec,
        out_shape=jax.ShapeDtypeStruct(q.shape, BF16),
        scratch_shapes=[pltpu.VMEM((KV_GROUP, kc, tq), F32), pltpu.VMEM((KV_GROUP, kc, tq), F32),
                        pltpu.VMEM((KV_GROUP, HEAD_DIM, tq), F32)],
        compiler_params=pltpu.CompilerParams(dimension_semantics=("parallel", "parallel", "parallel"),
                                             vmem_limit_bytes=V7X_VMEM_LIMIT_BYTES),
        name="attention",
    )(q, k, vt)


def _gla_scan(q_ref, k_ref, v_ref, la_ref, o_ref, state_ref, *, reverse):
    n = GLA_CHUNK
    row = lax.broadcasted_iota(jnp.int32, (n, n), 0)
    col = lax.broadcasted_iota(jnp.int32, (n, n), 1)
    if reverse:
        cum = (col >= row).astype(BF16)
        keep = col > row
        edge = 0
    else:
        cum = (col <= row).astype(BF16)
        keep = col <= row
        edge = n - 1
    n_chunks = q_ref.shape[0] // n
    order = range(n_chunks - 1, -1, -1) if reverse else range(n_chunks)
    for c in order:
        rows = slice(c * n, (c + 1) * n)
        la = la_ref[rows, :]
        la_hi = la.astype(BF16)
        la_lo = (la - la_hi.astype(F32)).astype(BF16)
        b = _dot(cum, la_hi) + _dot(cum, la_lo)
        b_edge = b[edge:edge + 1, :]
        q = q_ref[rows, :].astype(F32)
        k = k_ref[rows, :].astype(F32)
        v = v_ref[rows, :]
        q_dec = (q * jnp.exp(b)).astype(BF16)
        k_dec = (k * jnp.exp(-b)).astype(BF16)
        k_out = (k * jnp.exp(b_edge - b)).astype(BF16)
        a = jnp.where(keep, _dot_nt(q_dec, k_dec), 0.0).astype(BF16)
        state = state_ref[...]
        o_ref[rows, :] = _dot(a, v) + _dot_nt(q_dec, state.astype(BF16))
        state_ref[...] = state * jnp.exp(b_edge) + _dot_tn(v, k_out)


def _gla_kernel(qf_ref, kf_ref, vf_ref, laf_ref, qb_ref, kb_ref, vb_ref, lab_ref, of_ref, ob_ref, sf_ref, sb_ref):
    @pl.when(pl.program_id(2) == 0)
    def _():
        sf_ref[...] = jnp.zeros_like(sf_ref)
        sb_ref[...] = jnp.zeros_like(sb_ref)

    _gla_scan(qf_ref, kf_ref, vf_ref, laf_ref, of_ref, sf_ref, reverse=False)
    _gla_scan(qb_ref, kb_ref, vb_ref, lab_ref, ob_ref, sb_ref, reverse=True)


def _gla(q, k, v, la_f, la_b, batch, seq_len, tl):
    n_blk = seq_len // tl
    fwd = lambda w: pl.BlockSpec((tl, w), lambda b, h, i: (b * n_blk + i, h))
    bwd = lambda w: pl.BlockSpec((tl, w), lambda b, h, i: (b * n_blk + n_blk - 1 - i, h))
    out = jax.ShapeDtypeStruct((q.shape[0], GLA_V_WIDTH), F32)
    return pl.pallas_call(
        _gla_kernel,
        grid=(batch, N_GLA_HEADS, n_blk),
        in_specs=[fwd(GLA_DK), fwd(GLA_DK), fwd(GLA_DV), fwd(GLA_DK),
                  bwd(GLA_DK), bwd(GLA_DK), bwd(GLA_DV), bwd(GLA_DK)],
        out_specs=[fwd(GLA_DV), bwd(GLA_DV)],
        out_shape=[out, out],
        scratch_shapes=[pltpu.VMEM((GLA_DV, GLA_DK), F32), pltpu.VMEM((GLA_DV, GLA_DK), F32)],
        compiler_params=pltpu.CompilerParams(dimension_semantics=("parallel", "parallel", "arbitrary"),
                                             vmem_limit_bytes=V7X_VMEM_LIMIT_BYTES),
        name="gla",
    )(q, k, v, la_f, q, k, v, la_b)


def _outproj_kernel(attn_ref, of_ref, ob_ref, og_ref, x_ref, w_ref, ggla_ref, gpost_ref, y_ref):
    mix = _dot(attn_ref[...], w_ref[:ATTN_WIDTH, :])
    for hd in range(N_GLA_HEADS):
        cols = slice(hd * GLA_DV, (hd + 1) * GLA_DV)
        o = _rms(of_ref[:, cols] + ob_ref[:, cols], ggla_ref[...]) * og_ref[:, cols].astype(F32)
        mix += _dot(o.astype(BF16), w_ref[ATTN_WIDTH + hd * GLA_DV:ATTN_WIDTH + (hd + 1) * GLA_DV, :])
    y_ref[...] = x_ref[...] + _rms(mix, gpost_ref[...])


def _out_proj(attn, o_f, o_b, og, x, w_out, g_gla, g_post, tm):
    n_tok = x.shape[0]
    row = lambda w: pl.BlockSpec((tm, w), lambda i: (i, 0))
    whole = lambda a: pl.BlockSpec(a.shape, lambda i: (0,) * a.ndim)
    return pl.pallas_call(
        _outproj_kernel,
        grid=(n_tok // tm,),
        in_specs=[row(ATTN_WIDTH), row(GLA_V_WIDTH), row(GLA_V_WIDTH), row(GLA_V_WIDTH), row(D_MODEL),
                  whole(w_out), whole(g_gla), whole(g_post)],
        out_specs=row(D_MODEL),
        out_shape=jax.ShapeDtypeStruct(x.shape, F32),
        compiler_params=pltpu.CompilerParams(dimension_semantics=("parallel",),
                                             vmem_limit_bytes=V7X_VMEM_LIMIT_BYTES),
        name="out_proj",
    )(attn, o_f, o_b, og, x, w_out, g_gla, g_post)


def _ffn_kernel(x_ref, gpre_ref, wg_ref, wu_ref, wd_ref, gpost_ref, y_ref, h_ref, acc_ref):
    f = pl.program_id(1)

    @pl.when(f == 0)
    def _():
        h_ref[...] = _rms(x_ref[...], gpre_ref[...]).astype(BF16)

    h = h_ref[...]
    gate = _dot(h, wg_ref[...])
    up = _dot(h, wu_ref[...])
    part = _dot((gate * _sigmoid(gate) * up).astype(BF16), wd_ref[...])

    @pl.when(f == 0)
    def _():
        acc_ref[...] = part

    @pl.when(f > 0)
    def _():
        acc_ref[...] += part

    @pl.when(f == pl.num_programs(1) - 1)
    def _():
        y_ref[...] = x_ref[...] + _rms(acc_ref[...], gpost_ref[...])


def _ffn(x, g_pre, w_gate_up, w_down, g_post, tm, tf):
    n_tok = x.shape[0]
    n_f = D_FF // tf
    row = pl.BlockSpec((tm, D_MODEL), lambda i, f: (i, 0))
    vec = lambda a: pl.BlockSpec(a.shape, lambda i, f: (0,) * a.ndim)
    return pl.pallas_call(
        _ffn_kernel,
        grid=(n_tok // tm, n_f),
        in_specs=[row, vec(g_pre),
                  pl.BlockSpec((D_MODEL, tf), lambda i, f: (0, f)),
                  pl.BlockSpec((D_MODEL, tf), lambda i, f: (0, f + n_f)),
                  pl.BlockSpec((tf, D_MODEL), lambda i, f: (f, 0)),
                  vec(g_post)],
        out_specs=row,
        out_shape=jax.ShapeDtypeStruct(x.shape, F32),
        scratch_shapes=[pltpu.VMEM((tm, D_MODEL), BF16), pltpu.VMEM((tm, D_MODEL), F32)],
        compiler_params=pltpu.CompilerParams(dimension_semantics=("parallel", "arbitrary"),
                                             vmem_limit_bytes=V7X_VMEM_LIMIT_BYTES),
        name="ffn",
    )(x, g_pre, w_gate_up, w_gate_up, w_down, g_post)


def _ple_kernel(x_ref, p_ref, gpre_ref, wg_ref, wp_ref, gpost_ref, y_ref):
    x = x_ref[...]
    gate = _sigmoid(_dot(_rms(x, gpre_ref[...]).astype(BF16), wg_ref[...]))
    emb = _dot(p_ref[...].astype(BF16), wp_ref[...])
    y_ref[...] = x + _rms(emb * gate, gpost_ref[...])


def _ple(x, p, g_pre, w_gate, w_proj, g_post, tm):
    n_tok = x.shape[0]
    row = lambda w: pl.BlockSpec((tm, w), lambda i: (i, 0))
    whole = lambda a: pl.BlockSpec(a.shape, lambda i: (0,) * a.ndim)
    return pl.pallas_call(
        _ple_kernel,
        grid=(n_tok // tm,),
        in_specs=[row(D_MODEL), row(D_PLE), whole(g_pre), whole(w_gate), whole(w_proj), whole(g_post)],
        out_specs=row(D_MODEL),
        out_shape=jax.ShapeDtypeStruct(x.shape, F32),
        compiler_params=pltpu.CompilerParams(dimension_semantics=("parallel",),
                                             vmem_limit_bytes=V7X_VMEM_LIMIT_BYTES),
        name="ple",
    )(x, p, g_pre, w_gate, w_proj, g_post)


def _rope_tables(seq_len):
    t = jnp.arange(seq_len, dtype=jnp.int32)
    row = (t // GRID_W).astype(F32)
    col = (t % GRID_W).astype(F32)
    freqs = ROPE_THETA ** (-jnp.arange(0, ROPE_AXIS_DIM, 2, dtype=F32) / ROPE_AXIS_DIM)
    ang_r = row[:, None] * freqs[None, :]
    ang_c = col[:, None] * freqs[None, :]
    cos = jnp.concatenate([jnp.cos(ang_r), jnp.cos(ang_r), jnp.cos(ang_c), jnp.cos(ang_c)], axis=-1)
    sin = jnp.concatenate([-jnp.sin(ang_r), jnp.sin(ang_r), -jnp.sin(ang_c), jnp.sin(ang_c)], axis=-1)
    return cos, sin


def _layer(x3, p3, w, *, tm_in, tq, kc, tl, tm_out, tm_ffn, tf, tm_ple):
    batch, seq_len, _ = x3.shape
    x = x3.reshape(batch * seq_len, D_MODEL)
    p = p3.reshape(batch * seq_len, D_PLE)
    cos, sin = _rope_tables(seq_len)
    qa, ka, va, qg, kg, vg, og, la_f, la_b = _in_proj(
        x, w["g_pre_mix"], w["w_main"], w["w_low"], w["g_q"], w["g_k"], cos, sin, w["w_up"], w["b_up"],
        seq_len, tm_in)
    attn = _attention(qa, ka, va, batch, seq_len, tq, kc)
    o_f, o_b = _gla(qg, kg, vg, la_f, la_b, batch, seq_len, tl)
    x = _out_proj(attn, o_f, o_b, og, x, w["w_out"], w["g_gla_norm"], w["g_post_mix"], tm_out)
    x = _ffn(x, w["g_pre_ffn"], w["w_gate_up"], w["w_down"], w["g_post_ffn"], tm_ffn, tf)
    x = _ple(x, p, w["g_ple_pre"], w["w_ple_gate"], w["w_ple_proj"], w["g_ple_post"], tm_ple)
    return x.reshape(batch, seq_len, D_MODEL)


def _layer_weights(i, g_pre_mix, w_in, g_q, g_k, w_gf_up, b_gf, w_gb_up, b_gb, g_gla_norm, w_out, g_post_mix,
                   g_pre_ffn, w_gate_up, w_down, g_post_ffn, g_ple_pre, w_ple_gate, w_ple_proj, g_ple_post):
    vec = lambda g: g[i].astype(F32)[None, :]
    zeros = jnp.zeros((GATE_RANK, GLA_K_WIDTH), F32)
    w_up = jnp.concatenate([jnp.concatenate([w_gf_up[i], zeros], axis=1),
                            jnp.concatenate([zeros, w_gb_up[i]], axis=1)], axis=0)
    return {
        "g_pre_mix": vec(g_pre_mix), "w_main": w_in[i, :, :D_MAIN].astype(BF16),
        "w_low": w_in[i, :, D_MAIN:].astype(BF16), "g_q": vec(g_q), "g_k": vec(g_k),
        "w_up": w_up.astype(BF16), "b_up": jnp.concatenate([b_gf[i], b_gb[i]]).astype(F32)[None, :],
        "g_gla_norm": vec(g_gla_norm), "w_out": w_out[i].astype(BF16), "g_post_mix": vec(g_post_mix),
        "g_pre_ffn": vec(g_pre_ffn), "w_gate_up": w_gate_up[i].astype(BF16), "w_down": w_down[i].astype(BF16),
        "g_post_ffn": vec(g_post_ffn), "g_ple_pre": vec(g_ple_pre), "w_ple_gate": w_ple_gate[i].astype(BF16),
        "w_ple_proj": w_ple_proj[i].astype(BF16), "g_ple_post": vec(g_ple_post),
    }


_TILES = dict(tm_in=256, tq=512, kc=512, tl=256, tm_out=512, tm_ffn=512, tf=512, tm_ple=512)


def kernel(x_prompt, x_sample, p_prompt, p_sample, g_pre_mix, w_in, g_q, g_k, w_gf_up, b_gf, w_gb_up, b_gb,
           g_gla_norm, w_out, g_post_mix, g_pre_ffn, w_gate_up, w_down, g_post_ffn, g_ple_pre, w_ple_gate,
           w_ple_proj, g_ple_post):
    y_prompt = x_prompt
    y_sample = x_sample
    for i in range(g_pre_mix.shape[0]):
        w = _layer_weights(i, g_pre_mix, w_in, g_q, g_k, w_gf_up, b_gf, w_gb_up, b_gb, g_gla_norm, w_out,
                           g_post_mix, g_pre_ffn, w_gate_up, w_down, g_post_ffn, g_ple_pre, w_ple_gate,
                           w_ple_proj, g_ple_post)
        y_prompt = _layer(y_prompt, p_prompt[i], w, **_TILES)
        y_sample = _layer(y_sample, p_sample[i], w, **_TILES)
    return (y_prompt, y_sample)
```

```python
import functools

import jax
import jax.numpy as jnp
from jax import lax
from jax.experimental import pallas as pl
from jax.experimental.pallas import tpu as pltpu

D_MODEL = 2048
HEAD_DIM = 128
N_ATTN_HEADS = 8
N_KV_HEADS = 2
KV_GROUP = N_ATTN_HEADS // N_KV_HEADS
ATTN_WIDTH = N_ATTN_HEADS * HEAD_DIM
KV_WIDTH = N_KV_HEADS * HEAD_DIM
ROPE_THETA = 10000.0
ROPE_AXIS_DIM = HEAD_DIM // 2
GRID_W = 64
N_GLA_HEADS = 4
GLA_DK = 128
GLA_DV = 256
GLA_K_WIDTH = N_GLA_HEADS * GLA_DK
GLA_V_WIDTH = N_GLA_HEADS * GLA_DV
GATE_RANK = 16
GATE_TAU = 16.0
GLA_CHUNK = 64
MIX_WIDTH = ATTN_WIDTH + GLA_V_WIDTH
D_FF = 5632
D_PLE = 256
EPS = 1e-6

_OFF_QA = 0
_OFF_KA = _OFF_QA + ATTN_WIDTH
_OFF_VA = _OFF_KA + KV_WIDTH
_OFF_QG = _OFF_VA + KV_WIDTH
_OFF_KG = _OFF_QG + GLA_K_WIDTH
_OFF_VG = _OFF_KG + GLA_K_WIDTH
_OFF_OG = _OFF_VG + GLA_V_WIDTH
_OFF_LOW = _OFF_OG + GLA_V_WIDTH
D_MAIN = _OFF_LOW

V7X_VMEM_LIMIT_BYTES = 56 * 1024 * 1024

F32 = jnp.float32
BF16 = jnp.bfloat16

_Q_SCALE = HEAD_DIM ** -0.5 * 1.4426950408889634


def _rms(xf, g):
    ms = jnp.mean(xf * xf, axis=-1, keepdims=True)
    return xf * lax.rsqrt(ms + EPS) * g


def _dot(a, b):
    return jnp.dot(a, b, preferred_element_type=F32)


def _dot_nt(a, b):
    return lax.dot_general(a, b, (((1,), (1,)), ((), ())), preferred_element_type=F32)


def _dot_tn(a, b):
    return lax.dot_general(a, b, (((0,), (0,)), ((), ())), preferred_element_type=F32)


def _sigmoid(x):
    return 1.0 / (1.0 + jnp.exp(-x))


def _rope(z, cos, sin_signed, first_half):
    swapped = jnp.where(first_half, pltpu.roll(z, HEAD_DIM - 32, axis=1), pltpu.roll(z, 32, axis=1))
    return z * cos + swapped * sin_signed


def _inproj_kernel(x_ref, gpre_ref, w_ref, wlow_ref, gq_ref, gk_ref, cos_ref, sin_ref, wup_ref, bup_ref,
                   qa_ref, ka_ref, va_ref, qg_ref, kg_ref, vg_ref, og_ref, laf_ref, lab_ref):
    h = _rms(x_ref[...], gpre_ref[...]).astype(BF16)
    cos = cos_ref[...]
    sin = sin_ref[...]
    lane = lax.broadcasted_iota(jnp.int32, cos.shape, 1)
    first_half = (lane % ROPE_AXIS_DIM) < (ROPE_AXIS_DIM // 2)

    def qk_head(z, g, scale):
        return (_rope(_rms(z, g), cos, sin, first_half) * scale).astype(BF16)

    slab = 4 * HEAD_DIM
    for s in range(ATTN_WIDTH // slab):
        z = _dot(h, w_ref[:, _OFF_QA + s * slab:_OFF_QA + (s + 1) * slab])
        for j in range(slab // HEAD_DIM):
            c = s * slab + j * HEAD_DIM
            qa_ref[:, c:c + HEAD_DIM] = qk_head(z[:, j * HEAD_DIM:(j + 1) * HEAD_DIM], gq_ref[...], _Q_SCALE)

    z = _dot(h, w_ref[:, _OFF_KA:_OFF_KA + 2 * KV_WIDTH])
    for j in range(N_KV_HEADS):
        ka_ref[:, j * HEAD_DIM:(j + 1) * HEAD_DIM] = qk_head(z[:, j * HEAD_DIM:(j + 1) * HEAD_DIM], gk_ref[...], 1.0)
    va_ref[...] = z[:, KV_WIDTH:].astype(BF16)

    qg_ref[...] = (_dot(h, w_ref[:, _OFF_QG:_OFF_QG + GLA_K_WIDTH]) * (GLA_DK ** -0.5)).astype(BF16)
    kg_ref[...] = _dot(h, w_ref[:, _OFF_KG:_OFF_KG + GLA_K_WIDTH]).astype(BF16)
    half = GLA_V_WIDTH // 2
    for s in range(2):
        vg_ref[:, s * half:(s + 1) * half] = _dot(h, w_ref[:, _OFF_VG + s * half:_OFF_VG + (s + 1) * half]).astype(BF16)
    for s in range(2):
        og = _dot(h, w_ref[:, _OFF_OG + s * half:_OFF_OG + (s + 1) * half])
        og_ref[:, s * half:(s + 1) * half] = (og * _sigmoid(og)).astype(BF16)

    low = _dot(h, wlow_ref[...]).astype(BF16)
    pre = _dot(low, wup_ref[...]) + bup_ref[...]
    log_a = (jnp.minimum(pre, 0.0) - jnp.log(1.0 + jnp.exp(-jnp.abs(pre)))) * (1.0 / GATE_TAU)
    laf_ref[...] = log_a[:, :GLA_K_WIDTH]
    lab_ref[...] = log_a[:, GLA_K_WIDTH:]


def _in_proj(x, g_pre, w_main, w_low, g_q, g_k, cos, sin, w_up, b_up, seq_len, tm):
    n_tok = x.shape[0]
    n_pos_blocks = seq_len // tm
    row = lambda w: pl.BlockSpec((tm, w), lambda i: (i, 0))
    whole = lambda a: pl.BlockSpec(a.shape, lambda i: (0,) * a.ndim)
    pos = pl.BlockSpec((tm, HEAD_DIM), lambda i: (i % n_pos_blocks, 0))
    out_widths = (ATTN_WIDTH, KV_WIDTH, KV_WIDTH, GLA_K_WIDTH, GLA_K_WIDTH, GLA_V_WIDTH, GLA_V_WIDTH)
    out_shape = [jax.ShapeDtypeStruct((n_tok, w), BF16) for w in out_widths]
    out_shape += [jax.ShapeDtypeStruct((n_tok, GLA_K_WIDTH), F32)] * 2
    out_specs = [row(w) for w in out_widths] + [row(GLA_K_WIDTH)] * 2
    return pl.pallas_call(
        _inproj_kernel,
        grid=(n_tok // tm,),
        in_specs=[row(D_MODEL), whole(g_pre), whole(w_main), whole(w_low), whole(g_q), whole(g_k), pos, pos,
                  whole(w_up), whole(b_up)],
        out_specs=out_specs,
        out_shape=out_shape,
        compiler_params=pltpu.CompilerParams(dimension_semantics=("parallel",),
                                             vmem_limit_bytes=V7X_VMEM_LIMIT_BYTES),
        name="in_proj",
    )(x, g_pre, w_main, w_low, g_q, g_k, cos, sin, w_up, b_up)


_REDUCE_SLAB_ROWS = 32


def _col_reduce(x, reduce_fn):
    rows, n = x.shape
    slab = reduce_fn(x.reshape(rows // _REDUCE_SLAB_ROWS, _REDUCE_SLAB_ROWS, n), axis=0)
    return reduce_fn(slab, axis=0, keepdims=True)


def _attn_kernel(q_ref, k_ref, vt_ref, o_ref, st_even_ref, st_odd_ref, acc_ref):
    tq = q_ref.shape[0]
    n_chunks, _, kc = vt_ref.shape
    assert n_chunks % 2 == 0
    acc_ref[...] = jnp.zeros_like(acc_ref)

    def scores(c, st_ref, m_all):
        k = k_ref[pl.ds(pl.multiple_of(c * kc, kc), kc), :]
        m_out = []
        for g in range(KV_GROUP):
            st = _dot_nt(k, q_ref[:, g * HEAD_DIM:(g + 1) * HEAD_DIM])
            st_ref[g] = st
            m_out.append(jnp.maximum(m_all[g], _col_reduce(st, jnp.max)))
        return tuple(m_out)

    def accumulate(c, st_ref, m_old, m_new, l_all):
        vt = vt_ref[c]
        l_out = []
        for g in range(KV_GROUP):
            alpha = jnp.exp2(m_old[g] - m_new[g])
            p = jnp.exp2(st_ref[g] - m_new[g])
            l_out.append(alpha * l_all[g] + _col_reduce(p, jnp.sum))
            acc_ref[g] = alpha * acc_ref[g] + _dot(vt, p.astype(BF16))
        return tuple(l_out)

    def pair(i, carry):
        m_old, m_cur, l_all = carry
        c = 2 * i
        m_odd = scores(c + 1, st_odd_ref, m_cur)
        l_all = accumulate(c, st_even_ref, m_old, m_cur, l_all)
        m_even = scores(c + 2, st_even_ref, m_odd)
        l_all = accumulate(c + 1, st_odd_ref, m_cur, m_odd, l_all)
        return m_odd, m_even, l_all

    m_init = tuple(jnp.full((1, tq), -jnp.inf, F32) for _ in range(KV_GROUP))
    l_init = tuple(jnp.zeros((1, tq), F32) for _ in range(KV_GROUP))
    m_old, m_cur, l_all = lax.fori_loop(0, n_chunks // 2 - 1, pair,
                                        (m_init, scores(0, st_even_ref, m_init), l_init))
    m_last = scores(n_chunks - 1, st_odd_ref, m_cur)
    l_all = accumulate(n_chunks - 2, st_even_ref, m_old, m_cur, l_all)
    l_all = accumulate(n_chunks - 1, st_odd_ref, m_cur, m_last, l_all)
    for g in range(KV_GROUP):
        o_ref[:, g * HEAD_DIM:(g + 1) * HEAD_DIM] = (acc_ref[g] / l_all[g]).T.astype(o_ref.dtype)


def _attention(q, k, v, batch, seq_len, tq, kc):
    n_q = seq_len // tq
    n_chunks = seq_len // kc
    vt = v.reshape(batch, n_chunks, kc, N_KV_HEADS, HEAD_DIM).transpose(0, 3, 1, 4, 2)
    vt = vt.reshape(batch * N_KV_HEADS, n_chunks, HEAD_DIM, kc)
    q_spec = pl.BlockSpec((tq, KV_GROUP * HEAD_DIM), lambda b, h, i: (b * n_q + i, h))
    k_spec = pl.BlockSpec((seq_len, HEAD_DIM), lambda b, h, i: (b, h))
    vt_spec = pl.BlockSpec((None, n_chunks, HEAD_DIM, kc), lambda b, h, i: (b * N_KV_HEADS + h, 0, 0, 0))
    return pl.pallas_call(
        _attn_kernel,
        grid=(batch, N_KV_HEADS, n_q),
        in_specs=[q_spec, k_spec, vt_spec],
        out_specs=q_spec,
        out_shape=jax.ShapeDtypeStruct(q.shape, BF16),
        scratch_shapes=[pltpu.VMEM((KV_GROUP, kc, tq), F32), pltpu.VMEM((KV_GROUP, kc, tq), F32),
                        pltpu.VMEM((KV_GROUP, HEAD_DIM, tq), F32)],
        compiler_params=pltpu.CompilerParams(dimension_semantics=("parallel", "parallel", "parallel"),
                                             vmem_limit_bytes=V7X_VMEM_LIMIT_BYTES),
        name="attention",
    )(q, k, vt)


def _gla_scan(q_ref, k_ref, v_ref, la_ref, o_ref, state_ref, *, reverse):
    n = GLA_CHUNK
    tl = q_ref.shape[0]
    n_chunks = tl // n
    row = lax.broadcasted_iota(jnp.int32, (tl, tl), 0)
    col = lax.broadcasted_iota(jnp.int32, (tl, tl), 1)
    same_chunk = (row // n) == (col // n)
    if reverse:
        cum = (same_chunk & (col >= row)).astype(BF16)
        keep = same_chunk & (col > row)
        edge = 0
    else:
        cum = (same_chunk & (col <= row)).astype(BF16)
        keep = same_chunk & (col <= row)
        edge = n - 1
    for hd in range(q_ref.shape[1] // GLA_DK):
        kcols = slice(hd * GLA_DK, (hd + 1) * GLA_DK)
        vcols = slice(hd * GLA_DV, (hd + 1) * GLA_DV)
        la = la_ref[:, kcols]
        la_hi = la.astype(BF16)
        la_lo = (la - la_hi.astype(F32)).astype(BF16)
        b = _dot(cum, la_hi) + _dot(cum, la_lo)
        b_edge = [b[c * n + edge:c * n + edge + 1, :] for c in range(n_chunks)]
        b_edge_rows = jnp.concatenate([jnp.broadcast_to(e, (n, GLA_DK)) for e in b_edge], axis=0)
        q = q_ref[:, kcols].astype(F32)
        k = k_ref[:, kcols].astype(F32)
        v = v_ref[:, vcols]
        q_dec = (q * jnp.exp(b)).astype(BF16)
        k_dec = (k * jnp.exp(-b)).astype(BF16)
        k_out = (k * jnp.exp(b_edge_rows - b)).astype(BF16)
        a = jnp.where(keep, _dot_nt(q_dec, k_dec), 0.0).astype(BF16)
        o_intra = _dot(a, v)
        state = state_ref[hd]
        for c in (range(n_chunks - 1, -1, -1) if reverse else range(n_chunks)):
            rows = slice(c * n, (c + 1) * n)
            o_ref[rows, vcols] = o_intra[rows, :] + _dot_nt(q_dec[rows, :], state.astype(BF16))
            state = state * jnp.exp(b_edge[c]) + _dot_tn(v[rows, :], k_out[rows, :])
        state_ref[hd] = state


def _gla_kernel(qf_ref, kf_ref, vf_ref, laf_ref, qb_ref, kb_ref, vb_ref, lab_ref, of_ref, ob_ref, sf_ref, sb_ref):
    @pl.when(pl.program_id(2) == 0)
    def _():
        sf_ref[...] = jnp.zeros_like(sf_ref)
        sb_ref[...] = jnp.zeros_like(sb_ref)

    _gla_scan(qf_ref, kf_ref, vf_ref, laf_ref, of_ref, sf_ref, reverse=False)
    _gla_scan(qb_ref, kb_ref, vb_ref, lab_ref, ob_ref, sb_ref, reverse=True)


def _gla(q, k, v, la_f, la_b, batch, seq_len, tl, heads_per_step):
    n_blk = seq_len // tl
    hp = heads_per_step
    fwd = lambda w: pl.BlockSpec((tl, hp * w), lambda b, h, i: (b * n_blk + i, h))
    bwd = lambda w: pl.BlockSpec((tl, hp * w), lambda b, h, i: (b * n_blk + n_blk - 1 - i, h))
    out = jax.ShapeDtypeStruct((q.shape[0], GLA_V_WIDTH), F32)
    state = pltpu.VMEM((hp, GLA_DV, GLA_DK), F32)
    return pl.pallas_call(
        _gla_kernel,
        grid=(batch, N_GLA_HEADS // hp, n_blk),
        in_specs=[fwd(GLA_DK), fwd(GLA_DK), fwd(GLA_DV), fwd(GLA_DK),
                  bwd(GLA_DK), bwd(GLA_DK), bwd(GLA_DV), bwd(GLA_DK)],
        out_specs=[fwd(GLA_DV), bwd(GLA_DV)],
        out_shape=[out, out],
        scratch_shapes=[state, state],
        compiler_params=pltpu.CompilerParams(dimension_semantics=("parallel", "parallel", "arbitrary"),
                                             vmem_limit_bytes=V7X_VMEM_LIMIT_BYTES),
        name="gla",
    )(q, k, v, la_f, q, k, v, la_b)


def _outproj_kernel(attn_ref, of_ref, ob_ref, og_ref, x_ref, w_ref, ggla_ref, gpost_ref, y_ref):
    mix = _dot(attn_ref[...], w_ref[:ATTN_WIDTH, :])
    for hd in range(N_GLA_HEADS):
        cols = slice(hd * GLA_DV, (hd + 1) * GLA_DV)
        o = _rms(of_ref[:, cols] + ob_ref[:, cols], ggla_ref[...]) * og_ref[:, cols].astype(F32)
        mix += _dot(o.astype(BF16), w_ref[ATTN_WIDTH + hd * GLA_DV:ATTN_WIDTH + (hd + 1) * GLA_DV, :])
    y_ref[...] = x_ref[...] + _rms(mix, gpost_ref[...])


def _out_proj(attn, o_f, o_b, og, x, w_out, g_gla, g_post, tm):
    n_tok = x.shape[0]
    row = lambda w: pl.BlockSpec((tm, w), lambda i: (i, 0))
    whole = lambda a: pl.BlockSpec(a.shape, lambda i: (0,) * a.ndim)
    return pl.pallas_call(
        _outproj_kernel,
        grid=(n_tok // tm,),
        in_specs=[row(ATTN_WIDTH), row(GLA_V_WIDTH), row(GLA_V_WIDTH), row(GLA_V_WIDTH), row(D_MODEL),
                  whole(w_out), whole(g_gla), whole(g_post)],
        out_specs=row(D_MODEL),
        out_shape=jax.ShapeDtypeStruct(x.shape, F32),
        compiler_params=pltpu.CompilerParams(dimension_semantics=("parallel",),
                                             vmem_limit_bytes=V7X_VMEM_LIMIT_BYTES),
        name="out_proj",
    )(attn, o_f, o_b, og, x, w_out, g_gla, g_post)


def _ffn_kernel(x_ref, gpre_ref, wg_ref, wu_ref, wd_ref, gpost_ref, y_ref, h_ref, acc_ref):
    f = pl.program_id(1)

    @pl.when(f == 0)
    def _():
        h_ref[...] = _rms(x_ref[...], gpre_ref[...]).astype(BF16)
        acc_ref[...] = jnp.zeros_like(acc_ref)

    h = h_ref[...]
    gate = _dot(h, wg_ref[...])
    up = _dot(h, wu_ref[...])
    acc_ref[...] += _dot((gate * _sigmoid(gate) * up).astype(BF16), wd_ref[...])

    @pl.when(f == pl.num_programs(1) - 1)
    def _():
        y_ref[...] = x_ref[...] + _rms(acc_ref[...], gpost_ref[...])


def _ffn(x, g_pre, w_gate_up, w_down, g_post, tm, tf):
    n_tok = x.shape[0]
    n_f = D_FF // tf
    row = pl.BlockSpec((tm, D_MODEL), lambda i, f: (i, 0))
    vec = lambda a: pl.BlockSpec(a.shape, lambda i, f: (0,) * a.ndim)
    return pl.pallas_call(
        _ffn_kernel,
        grid=(n_tok // tm, n_f),
        in_specs=[row, vec(g_pre),
                  pl.BlockSpec((D_MODEL, tf), lambda i, f: (0, f)),
                  pl.BlockSpec((D_MODEL, tf), lambda i, f: (0, f + n_f)),
                  pl.BlockSpec((tf, D_MODEL), lambda i, f: (f, 0)),
                  vec(g_post)],
        out_specs=row,
        out_shape=jax.ShapeDtypeStruct(x.shape, F32),
        scratch_shapes=[pltpu.VMEM((tm, D_MODEL), BF16), pltpu.VMEM((tm, D_MODEL), F32)],
        compiler_params=pltpu.CompilerParams(dimension_semantics=("parallel", "arbitrary"),
                                             vmem_limit_bytes=V7X_VMEM_LIMIT_BYTES),
        name="ffn",
    )(x, g_pre, w_gate_up, w_gate_up, w_down, g_post)


def _ple_kernel(x_ref, p_ref, gpre_ref, wg_ref, wp_ref, gpost_ref, y_ref):
    x = x_ref[...]
    gate = _sigmoid(_dot(_rms(x, gpre_ref[...]).astype(BF16), wg_ref[...]))
    emb = _dot(p_ref[...].astype(BF16), wp_ref[...])
    y_ref[...] = x + _rms(emb * gate, gpost_ref[...])


def _ple(x, p, g_pre, w_gate, w_proj, g_post, tm):
    n_tok = x.shape[0]
    row = lambda w: pl.BlockSpec((tm, w), lambda i: (i, 0))
    whole = lambda a: pl.BlockSpec(a.shape, lambda i: (0,) * a.ndim)
    return pl.pallas_call(
        _ple_kernel,
        grid=(n_tok // tm,),
        in_specs=[row(D_MODEL), row(D_PLE), whole(g_pre), whole(w_gate), whole(w_proj), whole(g_post)],
        out_specs=row(D_MODEL),
        out_shape=jax.ShapeDtypeStruct(x.shape, F32),
        compiler_params=pltpu.CompilerParams(dimension_semantics=("parallel",),
                                             vmem_limit_bytes=V7X_VMEM_LIMIT_BYTES),
        name="ple",
    )(x, p, g_pre, w_gate, w_proj, g_post)


def _rope_tables(seq_len):
    t = jnp.arange(seq_len, dtype=jnp.int32)
    row = (t // GRID_W).astype(F32)
    col = (t % GRID_W).astype(F32)
    freqs = ROPE_THETA ** (-jnp.arange(0, ROPE_AXIS_DIM, 2, dtype=F32) / ROPE_AXIS_DIM)
    ang_r = row[:, None] * freqs[None, :]
    ang_c = col[:, None] * freqs[None, :]
    cos = jnp.concatenate([jnp.cos(ang_r), jnp.cos(ang_r), jnp.cos(ang_c), jnp.cos(ang_c)], axis=-1)
    sin = jnp.concatenate([-jnp.sin(ang_r), jnp.sin(ang_r), -jnp.sin(ang_c), jnp.sin(ang_c)], axis=-1)
    return cos, sin


def _layer(x3, p3, w, *, tm_in, tq, kc, tl, gla_heads, tm_out, tm_ffn, tf, tm_ple):
    batch, seq_len, _ = x3.shape
    x = x3.reshape(batch * seq_len, D_MODEL)
    p = p3.reshape(batch * seq_len, D_PLE)
    cos, sin = _rope_tables(seq_len)
    qa, ka, va, qg, kg, vg, og, la_f, la_b = _in_proj(
        x, w["g_pre_mix"], w["w_main"], w["w_low"], w["g_q"], w["g_k"], cos, sin, w["w_up"], w["b_up"],
        seq_len, tm_in)
    attn = _attention(qa, ka, va, batch, seq_len, tq, kc)
    o_f, o_b = _gla(qg, kg, vg, la_f, la_b, batch, seq_len, tl, gla_heads)
    x = _out_proj(attn, o_f, o_b, og, x, w["w_out"], w["g_gla_norm"], w["g_post_mix"], tm_out)
    x = _ffn(x, w["g_pre_ffn"], w["w_gate_up"], w["w_down"], w["g_post_ffn"], tm_ffn, tf)
    x = _ple(x, p, w["g_ple_pre"], w["w_ple_gate"], w["w_ple_proj"], w["g_ple_post"], tm_ple)
    return x.reshape(batch, seq_len, D_MODEL)


def _layer_weights(i, g_pre_mix, w_in, g_q, g_k, w_gf_up, b_gf, w_gb_up, b_gb, g_gla_norm, w_out, g_post_mix,
                   g_pre_ffn, w_gate_up, w_down, g_post_ffn, g_ple_pre, w_ple_gate, w_ple_proj, g_ple_post):
    vec = lambda g: g[i].astype(F32)[None, :]
    zeros = jnp.zeros((GATE_RANK, GLA_K_WIDTH), F32)
    w_up = jnp.concatenate([jnp.concatenate([w_gf_up[i], zeros], axis=1),
                            jnp.concatenate([zeros, w_gb_up[i]], axis=1)], axis=0)
    return {
        "g_pre_mix": vec(g_pre_mix), "w_main": w_in[i, :, :D_MAIN].astype(BF16),
        "w_low": w_in[i, :, D_MAIN:].astype(BF16), "g_q": vec(g_q), "g_k": vec(g_k),
        "w_up": w_up.astype(BF16), "b_up": jnp.concatenate([b_gf[i], b_gb[i]]).astype(F32)[None, :],
        "g_gla_norm": vec(g_gla_norm), "w_out": w_out[i].astype(BF16), "g_post_mix": vec(g_post_mix),
        "g_pre_ffn": vec(g_pre_ffn), "w_gate_up": w_gate_up[i].astype(BF16), "w_down": w_down[i].astype(BF16),
        "g_post_ffn": vec(g_post_ffn), "g_ple_pre": vec(g_ple_pre), "w_ple_gate": w_ple_gate[i].astype(BF16),
        "w_ple_proj": w_ple_proj[i].astype(BF16), "g_ple_post": vec(g_ple_post),
    }


_TILES = dict(tm_in=512, tq=1024, kc=512, tl=256, gla_heads=4, tm_out=512, tm_ffn=512, tf=512, tm_ple=512)


def kernel(x_prompt, x_sample, p_prompt, p_sample, g_pre_mix, w_in, g_q, g_k, w_gf_up, b_gf, w_gb_up, b_gb,
           g_gla_norm, w_out, g_post_mix, g_pre_ffn, w_gate_up, w_down, g_post_ffn, g_ple_pre, w_ple_gate,
           w_ple_proj, g_ple_post):
    y_prompt = x_prompt
    y_sample = x_sample
    for i in range(g_pre_mix.shape[0]):
        w = _layer_weights(i, g_pre_mix, w_in, g_q, g_k, w_gf_up, b_gf, w_gb_up, b_gb, g_gla_norm, w_out,
                           g_post_mix, g_pre_ffn, w_gate_up, w_down, g_post_ffn, g_ple_pre, w_ple_gate,
                           w_ple_proj, g_ple_post)
        y_prompt = _layer(y_prompt, p_prompt[i], w, **_TILES)
        y_sample = _layer(y_sample, p_sample[i], w, **_TILES)
    return (y_prompt, y_sample)
```

```python
import functools

import jax
import jax.numpy as jnp
from jax import lax
from jax.experimental import pallas as pl
from jax.experimental.pallas import tpu as pltpu

D_MODEL = 2048
HEAD_DIM = 128
N_ATTN_HEADS = 8
N_KV_HEADS = 2
KV_GROUP = N_ATTN_HEADS // N_KV_HEADS
ATTN_WIDTH = N_ATTN_HEADS * HEAD_DIM
KV_WIDTH = N_KV_HEADS * HEAD_DIM
ROPE_THETA = 10000.0
ROPE_AXIS_DIM = HEAD_DIM // 2
GRID_W = 64
N_GLA_HEADS = 4
GLA_DK = 128
GLA_DV = 256
GLA_K_WIDTH = N_GLA_HEADS * GLA_DK
GLA_V_WIDTH = N_GLA_HEADS * GLA_DV
GATE_RANK = 16
GATE_TAU = 16.0
GLA_CHUNK = 64
MIX_WIDTH = ATTN_WIDTH + GLA_V_WIDTH
D_FF = 5632
D_PLE = 256
EPS = 1e-6

_OFF_QA = 0
_OFF_KA = _OFF_QA + ATTN_WIDTH
_OFF_VA = _OFF_KA + KV_WIDTH
_OFF_QG = _OFF_VA + KV_WIDTH
_OFF_KG = _OFF_QG + GLA_K_WIDTH
_OFF_VG = _OFF_KG + GLA_K_WIDTH
_OFF_OG = _OFF_VG + GLA_V_WIDTH
_OFF_LOW = _OFF_OG + GLA_V_WIDTH

V7X_VMEM_LIMIT_BYTES = 56 * 1024 * 1024

F32 = jnp.float32
BF16 = jnp.bfloat16

_Q_SCALE = HEAD_DIM ** -0.5 * 1.4426950408889634


def _rms(xf, g):
    ms = jnp.mean(xf * xf, axis=-1, keepdims=True)
    return xf * lax.rsqrt(ms + EPS) * g


def _dot(a, b):
    return jnp.dot(a, b, preferred_element_type=F32)


def _dot_nt(a, b):
    return lax.dot_general(a, b, (((1,), (1,)), ((), ())), preferred_element_type=F32)


def _dot_tn(a, b):
    return lax.dot_general(a, b, (((0,), (0,)), ((), ())), preferred_element_type=F32)


def _sigmoid(x):
    return 1.0 / (1.0 + jnp.exp(-x))


def _rope(z, cos, sin_signed, first_half):
    swapped = jnp.where(first_half, pltpu.roll(z, HEAD_DIM - 32, axis=1), pltpu.roll(z, 32, axis=1))
    return z * cos + swapped * sin_signed


def _inproj_kernel(x_ref, gpre_ref, w_ref, gq_ref, gk_ref, cos_ref, sin_ref, wup_ref, bup_ref,
                   qa_ref, ka_ref, va_ref, qg_ref, kg_ref, vg_ref, og_ref, laf_ref, lab_ref):
    h = _rms(x_ref[...], gpre_ref[...]).astype(BF16)
    cos = cos_ref[...]
    sin = sin_ref[...]
    lane = lax.broadcasted_iota(jnp.int32, cos.shape, 1)
    first_half = (lane % ROPE_AXIS_DIM) < (ROPE_AXIS_DIM // 2)

    def qk_head(z, g, scale):
        return (_rope(_rms(z, g), cos, sin, first_half) * scale).astype(BF16)

    slab = 4 * HEAD_DIM
    for s in range(ATTN_WIDTH // slab):
        z = _dot(h, w_ref[:, _OFF_QA + s * slab:_OFF_QA + (s + 1) * slab])
        for j in range(slab // HEAD_DIM):
            c = s * slab + j * HEAD_DIM
            qa_ref[:, c:c + HEAD_DIM] = qk_head(z[:, j * HEAD_DIM:(j + 1) * HEAD_DIM], gq_ref[...], _Q_SCALE)

    z = _dot(h, w_ref[:, _OFF_KA:_OFF_KA + 2 * KV_WIDTH])
    for j in range(N_KV_HEADS):
        ka_ref[:, j * HEAD_DIM:(j + 1) * HEAD_DIM] = qk_head(z[:, j * HEAD_DIM:(j + 1) * HEAD_DIM], gk_ref[...], 1.0)
    va_ref[...] = z[:, KV_WIDTH:].astype(BF16)

    qg_ref[...] = (_dot(h, w_ref[:, _OFF_QG:_OFF_QG + GLA_K_WIDTH]) * (GLA_DK ** -0.5)).astype(BF16)
    kg_ref[...] = _dot(h, w_ref[:, _OFF_KG:_OFF_KG + GLA_K_WIDTH]).astype(BF16)
    half = GLA_V_WIDTH // 2
    for s in range(2):
        vg_ref[:, s * half:(s + 1) * half] = _dot(h, w_ref[:, _OFF_VG + s * half:_OFF_VG + (s + 1) * half]).astype(BF16)
    for s in range(2):
        og = _dot(h, w_ref[:, _OFF_OG + s * half:_OFF_OG + (s + 1) * half])
        og_ref[:, s * half:(s + 1) * half] = (og * _sigmoid(og)).astype(BF16)

    low = _dot(h, w_ref[:, _OFF_LOW:]).astype(BF16)
    pre = _dot(low, wup_ref[...]) + bup_ref[...]
    log_a = (jnp.minimum(pre, 0.0) - jnp.log(1.0 + jnp.exp(-jnp.abs(pre)))) * (1.0 / GATE_TAU)
    laf_ref[...] = log_a[:, :GLA_K_WIDTH]
    lab_ref[...] = log_a[:, GLA_K_WIDTH:]


def _in_proj(x, g_pre, w_in, g_q, g_k, cos, sin, w_up, b_up, seq_len, tm):
    n_tok = x.shape[0]
    n_pos_blocks = seq_len // tm
    row = lambda w: pl.BlockSpec((tm, w), lambda i: (i, 0))
    whole = lambda a: pl.BlockSpec(a.shape, lambda i: (0,) * a.ndim)
    pos = pl.BlockSpec((tm, HEAD_DIM), lambda i: (i % n_pos_blocks, 0))
    out_widths = (ATTN_WIDTH, KV_WIDTH, KV_WIDTH, GLA_K_WIDTH, GLA_K_WIDTH, GLA_V_WIDTH, GLA_V_WIDTH)
    out_shape = [jax.ShapeDtypeStruct((n_tok, w), BF16) for w in out_widths]
    out_shape += [jax.ShapeDtypeStruct((n_tok, GLA_K_WIDTH), F32)] * 2
    out_specs = [row(w) for w in out_widths] + [row(GLA_K_WIDTH)] * 2
    return pl.pallas_call(
        _inproj_kernel,
        grid=(n_tok // tm,),
        in_specs=[row(D_MODEL), whole(g_pre), whole(w_in), whole(g_q), whole(g_k), pos, pos,
                  whole(w_up), whole(b_up)],
        out_specs=out_specs,
        out_shape=out_shape,
        compiler_params=pltpu.CompilerParams(dimension_semantics=("parallel",),
                                             vmem_limit_bytes=V7X_VMEM_LIMIT_BYTES),
        name="in_proj",
    )(x, g_pre, w_in, g_q, g_k, cos, sin, w_up, b_up)


_REDUCE_SLAB_ROWS = 32


def _col_reduce(x, reduce_fn):
    rows, n = x.shape
    slab = reduce_fn(x.reshape(rows // _REDUCE_SLAB_ROWS, _REDUCE_SLAB_ROWS, n), axis=0)
    return reduce_fn(slab, axis=0, keepdims=True)


def _attn_kernel(q_ref, k_ref, vt_ref, o_ref, st_even_ref, st_odd_ref, acc_ref):
    tq = q_ref.shape[0]
    n_chunks, _, kc = vt_ref.shape
    assert n_chunks % 2 == 0
    acc_ref[...] = jnp.zeros_like(acc_ref)

    def scores(c, st_ref, m_all):
        k = k_ref[pl.ds(pl.multiple_of(c * kc, kc), kc), :]
        m_out = []
        for g in range(KV_GROUP):
            st = _dot_nt(k, q_ref[:, g * HEAD_DIM:(g + 1) * HEAD_DIM])
            st_ref[g] = st
            m_out.append(jnp.maximum(m_all[g], _col_reduce(st, jnp.max)))
        return tuple(m_out)

    def accumulate(c, st_ref, m_old, m_new, l_all):
        vt = vt_ref[c]
        l_out = []
        for g in range(KV_GROUP):
            alpha = jnp.exp2(m_old[g] - m_new[g])
            p = jnp.exp2(st_ref[g] - m_new[g])
            l_out.append(alpha * l_all[g] + _col_reduce(p, jnp.sum))
            acc_ref[g] = alpha * acc_ref[g] + _dot(vt, p.astype(BF16))
        return tuple(l_out)

    def pair(i, carry):
        m_old, m_cur, l_all = carry
        c = 2 * i
        m_odd = scores(c + 1, st_odd_ref, m_cur)
        l_all = accumulate(c, st_even_ref, m_old, m_cur, l_all)
        m_even = scores(c + 2, st_even_ref, m_odd)
        l_all = accumulate(c + 1, st_odd_ref, m_cur, m_odd, l_all)
        return m_odd, m_even, l_all

    m_init = tuple(jnp.full((1, tq), -jnp.inf, F32) for _ in range(KV_GROUP))
    l_init = tuple(jnp.zeros((1, tq), F32) for _ in range(KV_GROUP))
    m_old, m_cur, l_all = lax.fori_loop(0, n_chunks // 2 - 1, pair,
                                        (m_init, scores(0, st_even_ref, m_init), l_init))
    m_last = scores(n_chunks - 1, st_odd_ref, m_cur)
    l_all = accumulate(n_chunks - 2, st_even_ref, m_old, m_cur, l_all)
    l_all = accumulate(n_chunks - 1, st_odd_ref, m_cur, m_last, l_all)
    for g in range(KV_GROUP):
        o_ref[:, g * HEAD_DIM:(g + 1) * HEAD_DIM] = (acc_ref[g] / l_all[g]).T.astype(o_ref.dtype)


def _attention(q, k, v, batch, seq_len, tq, kc):
    n_q = seq_len // tq
    n_chunks = seq_len // kc
    vt = v.reshape(batch, n_chunks, kc, N_KV_HEADS, HEAD_DIM).transpose(0, 3, 1, 4, 2)
    vt = vt.reshape(batch * N_KV_HEADS, n_chunks, HEAD_DIM, kc)
    q_spec = pl.BlockSpec((tq, KV_GROUP * HEAD_DIM), lambda b, h, i: (b * n_q + i, h))
    k_spec = pl.BlockSpec((seq_len, HEAD_DIM), lambda b, h, i: (b, h))
    vt_spec = pl.BlockSpec((None, n_chunks, HEAD_DIM, kc), lambda b, h, i: (b * N_KV_HEADS + h, 0, 0, 0))
    return pl.pallas_call(
        _attn_kernel,
        grid=(batch, N_KV_HEADS, n_q),
        in_specs=[q_spec, k_spec, vt_spec],
        out_specs=q_spec,
        out_shape=jax.ShapeDtypeStruct(q.shape, BF16),
        scratch_shapes=[pltpu.VMEM((KV_GROUP, kc, tq), F32), pltpu.VMEM((KV_GROUP, kc, tq), F32),
                        pltpu.VMEM((KV_GROUP, HEAD_DIM, tq), F32)],
        compiler_params=pltpu.CompilerParams(dimension_semantics=("parallel", "parallel", "parallel"),
                                             vmem_limit_bytes=V7X_VMEM_LIMIT_BYTES),
        name="attention",
    )(q, k, vt)


def _gla_scan(q_ref, k_ref, v_ref, la_ref, o_ref, state_ref, *, reverse):
    n = GLA_CHUNK
    tl = q_ref.shape[0]
    n_chunks = tl // n
    row = lax.broadcasted_iota(jnp.int32, (tl, tl), 0)
    col = lax.broadcasted_iota(jnp.int32, (tl, tl), 1)
    same_chunk = (row // n) == (col // n)
    if reverse:
        cum = (same_chunk & (col >= row)).astype(BF16)
        keep = same_chunk & (col > row)
        edge = 0
    else:
        cum = (same_chunk & (col <= row)).astype(BF16)
        keep = same_chunk & (col <= row)
        edge = n - 1
    for hd in range(q_ref.shape[1] // GLA_DK):
        kcols = slice(hd * GLA_DK, (hd + 1) * GLA_DK)
        vcols = slice(hd * GLA_DV, (hd + 1) * GLA_DV)
        la = la_ref[:, kcols]
        la_hi = la.astype(BF16)
        la_lo = (la - la_hi.astype(F32)).astype(BF16)
        b = _dot(cum, la_hi) + _dot(cum, la_lo)
        b_edge = [b[c * n + edge:c * n + edge + 1, :] for c in range(n_chunks)]
        b_edge_rows = jnp.concatenate([jnp.broadcast_to(e, (n, GLA_DK)) for e in b_edge], axis=0)
        q = q_ref[:, kcols].astype(F32)
        k = k_ref[:, kcols].astype(F32)
        v = v_ref[:, vcols]
        q_dec = (q * jnp.exp(b)).astype(BF16)
        k_dec = (k * jnp.exp(-b)).astype(BF16)
        k_out = (k * jnp.exp(b_edge_rows - b)).astype(BF16)
        a = jnp.where(keep, _dot_nt(q_dec, k_dec), 0.0).astype(BF16)
        o_intra = _dot(a, v)
        state = state_ref[hd]
        for c in (range(n_chunks - 1, -1, -1) if reverse else range(n_chunks)):
            rows = slice(c * n, (c + 1) * n)
            o_ref[rows, vcols] = o_intra[rows, :] + _dot_nt(q_dec[rows, :], state.astype(BF16))
            state = state * jnp.exp(b_edge[c]) + _dot_tn(v[rows, :], k_out[rows, :])
        state_ref[hd] = state


def _gla_kernel(qf_ref, kf_ref, vf_ref, laf_ref, qb_ref, kb_ref, vb_ref, lab_ref, of_ref, ob_ref, sf_ref, sb_ref):
    @pl.when(pl.program_id(2) == 0)
    def _():
        sf_ref[...] = jnp.zeros_like(sf_ref)
        sb_ref[...] = jnp.zeros_like(sb_ref)

    _gla_scan(qf_ref, kf_ref, vf_ref, laf_ref, of_ref, sf_ref, reverse=False)
    _gla_scan(qb_ref, kb_ref, vb_ref, lab_ref, ob_ref, sb_ref, reverse=True)


def _gla(q, k, v, la_f, la_b, batch, seq_len, tl, heads_per_step):
    n_blk = seq_len // tl
    hp = heads_per_step
    fwd = lambda w: pl.BlockSpec((tl, hp * w), lambda b, h, i: (b * n_blk + i, h))
    bwd = lambda w: pl.BlockSpec((tl, hp * w), lambda b, h, i: (b * n_blk + n_blk - 1 - i, h))
    out = jax.ShapeDtypeStruct((q.shape[0], GLA_V_WIDTH), F32)
    state = pltpu.VMEM((hp, GLA_DV, GLA_DK), F32)
    return pl.pallas_call(
        _gla_kernel,
        grid=(batch, N_GLA_HEADS // hp, n_blk),
        in_specs=[fwd(GLA_DK), fwd(GLA_DK), fwd(GLA_DV), fwd(GLA_DK),
                  bwd(GLA_DK), bwd(GLA_DK), bwd(GLA_DV), bwd(GLA_DK)],
        out_specs=[fwd(GLA_DV), bwd(GLA_DV)],
        out_shape=[out, out],
        scratch_shapes=[state, state],
        compiler_params=pltpu.CompilerParams(dimension_semantics=("parallel", "parallel", "arbitrary"),
                                             vmem_limit_bytes=V7X_VMEM_LIMIT_BYTES),
        name="gla",
    )(q, k, v, la_f, q, k, v, la_b)


def _outproj_kernel(attn_ref, of_ref, ob_ref, og_ref, x_ref, w_ref, ggla_ref, gpost_ref, gffn_ref, y_ref, h_ref):
    mix = _dot(attn_ref[...], w_ref[:ATTN_WIDTH, :])
    for hd in range(N_GLA_HEADS):
        cols = slice(hd * GLA_DV, (hd + 1) * GLA_DV)
        o = _rms(of_ref[:, cols] + ob_ref[:, cols], ggla_ref[...]) * og_ref[:, cols].astype(F32)
        mix += _dot(o.astype(BF16), w_ref[ATTN_WIDTH + hd * GLA_DV:ATTN_WIDTH + (hd + 1) * GLA_DV, :])
    y = x_ref[...] + _rms(mix, gpost_ref[...])
    y_ref[...] = y
    h_ref[...] = _rms(y, gffn_ref[...]).astype(BF16)


def _out_proj(attn, o_f, o_b, og, x, w_out, g_gla, g_post, g_pre_ffn, tm):
    n_tok = x.shape[0]
    row = lambda w: pl.BlockSpec((tm, w), lambda i: (i, 0))
    whole = lambda a: pl.BlockSpec(a.shape, lambda i: (0,) * a.ndim)
    return pl.pallas_call(
        _outproj_kernel,
        grid=(n_tok // tm,),
        in_specs=[row(ATTN_WIDTH), row(GLA_V_WIDTH), row(GLA_V_WIDTH), row(GLA_V_WIDTH), row(D_MODEL),
                  whole(w_out), whole(g_gla), whole(g_post), whole(g_pre_ffn)],
        out_specs=[row(D_MODEL), row(D_MODEL)],
        out_shape=[jax.ShapeDtypeStruct(x.shape, F32), jax.ShapeDtypeStruct(x.shape, BF16)],
        compiler_params=pltpu.CompilerParams(dimension_semantics=("parallel",),
                                             vmem_limit_bytes=V7X_VMEM_LIMIT_BYTES),
        name="out_proj",
    )(attn, o_f, o_b, og, x, w_out, g_gla, g_post, g_pre_ffn)


def _ffn_kernel(h_ref, wg_ref, wu_ref, wd_ref, y_ref):
    @pl.when(pl.program_id(1) == 0)
    def _():
        y_ref[...] = jnp.zeros_like(y_ref)

    h = h_ref[...]
    gate = _dot(h, wg_ref[...])
    up = _dot(h, wu_ref[...])
    y_ref[...] += _dot((gate * _sigmoid(gate) * up).astype(BF16), wd_ref[...])


def _ffn(h, w_gate_up, w_down, tm, tf):
    n_tok = h.shape[0]
    n_f = D_FF // tf
    row = pl.BlockSpec((tm, D_MODEL), lambda i, f: (i, 0))
    return pl.pallas_call(
        _ffn_kernel,
        grid=(n_tok // tm, n_f),
        in_specs=[row,
                  pl.BlockSpec((D_MODEL, tf), lambda i, f: (0, f)),
                  pl.BlockSpec((D_MODEL, tf), lambda i, f: (0, f + n_f)),
                  pl.BlockSpec((tf, D_MODEL), lambda i, f: (f, 0))],
        out_specs=row,
        out_shape=jax.ShapeDtypeStruct(h.shape, F32),
        compiler_params=pltpu.CompilerParams(dimension_semantics=("parallel", "arbitrary"),
                                             vmem_limit_bytes=V7X_VMEM_LIMIT_BYTES),
        name="ffn",
    )(h, w_gate_up, w_gate_up, w_down)


def _ple_kernel(x_ref, ffn_ref, p_ref, gffn_ref, gpre_ref, wg_ref, wp_ref, gpost_ref, y_ref):
    x = x_ref[...] + _rms(ffn_ref[...], gffn_ref[...])
    gate = _sigmoid(_dot(_rms(x, gpre_ref[...]).astype(BF16), wg_ref[...]))
    emb = _dot(p_ref[...].astype(BF16), wp_ref[...])
    y_ref[...] = x + _rms(emb * gate, gpost_ref[...])


def _ple(x, ffn_out, p, g_post_ffn, g_pre, w_gate, w_proj, g_post, tm):
    n_tok = x.shape[0]
    row = lambda w: pl.BlockSpec((tm, w), lambda i: (i, 0))
    whole = lambda a: pl.BlockSpec(a.shape, lambda i: (0,) * a.ndim)
    return pl.pallas_call(
        _ple_kernel,
        grid=(n_tok // tm,),
        in_specs=[row(D_MODEL), row(D_MODEL), row(D_PLE), whole(g_post_ffn), whole(g_pre), whole(w_gate),
                  whole(w_proj), whole(g_post)],
        out_specs=row(D_MODEL),
        out_shape=jax.ShapeDtypeStruct(x.shape, F32),
        compiler_params=pltpu.CompilerParams(dimension_semantics=("parallel",),
                                             vmem_limit_bytes=V7X_VMEM_LIMIT_BYTES),
        name="ple",
    )(x, ffn_out, p, g_post_ffn, g_pre, w_gate, w_proj, g_post)


def _rope_tables(seq_len):
    n_rows = seq_len // GRID_W
    freqs = ROPE_THETA ** (-jnp.arange(0, ROPE_AXIS_DIM, 2, dtype=F32) / ROPE_AXIS_DIM)
    ang_r = jnp.arange(n_rows, dtype=F32)[:, None] * freqs[None, :]
    ang_c = jnp.arange(GRID_W, dtype=F32)[:, None] * freqs[None, :]
    by_row = lambda a: jnp.repeat(a, GRID_W, axis=0)
    by_col = lambda a: jnp.tile(a, (n_rows, 1))
    cos_r, sin_r, cos_c, sin_c = by_row(jnp.cos(ang_r)), by_row(jnp.sin(ang_r)), by_col(jnp.cos(ang_c)), by_col(jnp.sin(ang_c))
    cos = jnp.concatenate([cos_r, cos_r, cos_c, cos_c], axis=-1)
    sin = jnp.concatenate([-sin_r, sin_r, -sin_c, sin_c], axis=-1)
    return cos, sin


def _layer(x3, p3, w, *, tm_in, tq, kc, tl, gla_heads, tm_out, tm_ffn, tf, tm_ple):
    batch, seq_len, _ = x3.shape
    x = x3.reshape(batch * seq_len, D_MODEL)
    p = p3.reshape(batch * seq_len, D_PLE)
    cos, sin = _rope_tables(seq_len)
    qa, ka, va, qg, kg, vg, og, la_f, la_b = _in_proj(
        x, w["g_pre_mix"], w["w_in"], w["g_q"], w["g_k"], cos, sin, w["w_up"], w["b_up"],
        seq_len, tm_in)
    attn = _attention(qa, ka, va, batch, seq_len, tq, kc)
    o_f, o_b = _gla(qg, kg, vg, la_f, la_b, batch, seq_len, tl, gla_heads)
    x, h = _out_proj(attn, o_f, o_b, og, x, w["w_out"], w["g_gla_norm"], w["g_post_mix"], w["g_pre_ffn"], tm_out)
    ffn_out = _ffn(h, w["w_gate_up"], w["w_down"], tm_ffn, tf)
    x = _ple(x, ffn_out, p, w["g_post_ffn"], w["g_ple_pre"], w["w_ple_gate"], w["w_ple_proj"], w["g_ple_post"],
             tm_ple)
    return x.reshape(batch, seq_len, D_MODEL)


def _layer_weights(i, g_pre_mix, w_in, g_q, g_k, w_gf_up, b_gf, w_gb_up, b_gb, g_gla_norm, w_out, g_post_mix,
                   g_pre_ffn, w_gate_up, w_down, g_post_ffn, g_ple_pre, w_ple_gate, w_ple_proj, g_ple_post):
    vec = lambda g: g[i].astype(F32)[None, :]
    zeros = jnp.zeros((GATE_RANK, GLA_K_WIDTH), F32)
    w_up = jnp.concatenate([jnp.concatenate([w_gf_up[i], zeros], axis=1),
                            jnp.concatenate([zeros, w_gb_up[i]], axis=1)], axis=0)
    return {
        "g_pre_mix": vec(g_pre_mix), "w_in": w_in[i].astype(BF16), "g_q": vec(g_q), "g_k": vec(g_k),
        "w_up": w_up.astype(BF16), "b_up": jnp.concatenate([b_gf[i], b_gb[i]]).astype(F32)[None, :],
        "g_gla_norm": vec(g_gla_norm), "w_out": w_out[i].astype(BF16), "g_post_mix": vec(g_post_mix),
        "g_pre_ffn": vec(g_pre_ffn), "w_gate_up": w_gate_up[i].astype(BF16), "w_down": w_down[i].astype(BF16),
        "g_post_ffn": vec(g_post_ffn), "g_ple_pre": vec(g_ple_pre), "w_ple_gate": w_ple_gate[i].astype(BF16),
        "w_ple_proj": w_ple_proj[i].astype(BF16), "g_ple_post": vec(g_ple_post),
    }


_TILES = dict(tm_in=512, tq=1024, kc=512, tl=256, gla_heads=4, tm_out=512, tm_ffn=1024, tf=512, tm_ple=512)


def kernel(x_prompt, x_sample, p_prompt, p_sample, g_pre_mix, w_in, g_q, g_k, w_gf_up, b_gf, w_gb_up, b_gb,
           g_gla_norm, w_out, g_post_mix, g_pre_ffn, w_gate_up, w_down, g_post_ffn, g_ple_pre, w_ple_gate,
           w_ple_proj, g_ple_post):
    y_prompt = x_prompt
    y_sample = x_sample
    for i in range(g_pre_mix.shape[0]):
        w = _layer_weights(i, g_pre_mix, w_in, g_q, g_k, w_gf_up, b_gf, w_gb_up, b_gb, g_gla_norm, w_out,
                           g_post_mix, g_pre_ffn, w_gate_up, w_down, g_post_ffn, g_ple_pre, w_ple_gate,
                           w_ple_proj, g_ple_post)
        y_prompt = _layer(y_prompt, p_prompt[i], w, **_TILES)
        y_sample = _layer(y_sample, p_sample[i], w, **_TILES)
    return (y_prompt, y_sample)
```

```python
import functools

import jax
import jax.numpy as jnp
from jax import lax
from jax.experimental import pallas as pl
from jax.experimental.pallas import tpu as pltpu

D_MODEL = 2048
HEAD_DIM = 128
N_ATTN_HEADS = 8
N_KV_HEADS = 2
KV_GROUP = N_ATTN_HEADS // N_KV_HEADS
ATTN_WIDTH = N_ATTN_HEADS * HEAD_DIM
KV_WIDTH = N_KV_HEADS * HEAD_DIM
ROPE_THETA = 10000.0
ROPE_AXIS_DIM = HEAD_DIM // 2
GRID_W = 64
N_GLA_HEADS = 4
GLA_DK = 128
GLA_DV = 256
GLA_K_WIDTH = N_GLA_HEADS * GLA_DK
GLA_V_WIDTH = N_GLA_HEADS * GLA_DV
GATE_RANK = 16
GATE_TAU = 16.0
GLA_CHUNK = 64
MIX_WIDTH = ATTN_WIDTH + GLA_V_WIDTH
D_FF = 5632
D_PLE = 256
EPS = 1e-6

_OFF_QA = 0
_OFF_KA = _OFF_QA + ATTN_WIDTH
_OFF_VA = _OFF_KA + KV_WIDTH
_OFF_QG = _OFF_VA + KV_WIDTH
_OFF_KG = _OFF_QG + GLA_K_WIDTH
_OFF_VG = _OFF_KG + GLA_K_WIDTH
_OFF_OG = _OFF_VG + GLA_V_WIDTH
_OFF_LOW = _OFF_OG + GLA_V_WIDTH

V7X_VMEM_LIMIT_BYTES = 56 * 1024 * 1024

F32 = jnp.float32
BF16 = jnp.bfloat16

_Q_SCALE = HEAD_DIM ** -0.5 * 1.4426950408889634


def _rms(xf, g):
    ms = jnp.mean(xf * xf, axis=-1, keepdims=True)
    return xf * lax.rsqrt(ms + EPS) * g


def _dot(a, b):
    return jnp.dot(a, b, preferred_element_type=F32)


def _dot_nt(a, b):
    return lax.dot_general(a, b, (((1,), (1,)), ((), ())), preferred_element_type=F32)


def _dot_tn(a, b):
    return lax.dot_general(a, b, (((0,), (0,)), ((), ())), preferred_element_type=F32)


def _sigmoid(x):
    return 1.0 / (1.0 + jnp.exp(-x))


def _rope(z, cos, sin_signed, first_half):
    swapped = jnp.where(first_half, pltpu.roll(z, HEAD_DIM - 32, axis=1), pltpu.roll(z, 32, axis=1))
    return z * cos + swapped * sin_signed


def _inproj_kernel(x_ref, gpre_ref, w_ref, gq_ref, gk_ref, cos_ref, sin_ref, wup_ref, bup_ref,
                   qa_ref, ka_ref, va_ref, qg_ref, kg_ref, vg_ref, og_ref, laf_ref, lab_ref):
    h = _rms(x_ref[...], gpre_ref[...]).astype(BF16)
    cos = cos_ref[...]
    sin = sin_ref[...]
    lane = lax.broadcasted_iota(jnp.int32, cos.shape, 1)
    first_half = (lane % ROPE_AXIS_DIM) < (ROPE_AXIS_DIM // 2)

    def qk_head(z, g, scale):
        return (_rope(_rms(z, g), cos, sin, first_half) * scale).astype(BF16)


    low = _dot(h, w_ref[:, _OFF_LOW:]).astype(BF16)
    pre = _dot(low, wup_ref[...]) + bup_ref[...]
    log_a = (jnp.minimum(pre, 0.0) - jnp.log(1.0 + jnp.exp(-jnp.abs(pre)))) * (1.0 / GATE_TAU)
    laf_ref[...] = log_a[:, :GLA_K_WIDTH]
    lab_ref[...] = log_a[:, GLA_K_WIDTH:]

    slab = 4 * HEAD_DIM
    for s in range(ATTN_WIDTH // slab):
        z = _dot(h, w_ref[:, _OFF_QA + s * slab:_OFF_QA + (s + 1) * slab])
        for j in range(slab // HEAD_DIM):
            c = s * slab + j * HEAD_DIM
            qa_ref[:, c:c + HEAD_DIM] = qk_head(z[:, j * HEAD_DIM:(j + 1) * HEAD_DIM], gq_ref[...], _Q_SCALE)

    z = _dot(h, w_ref[:, _OFF_KA:_OFF_KA + 2 * KV_WIDTH])
    for j in range(N_KV_HEADS):
        ka_ref[:, j * HEAD_DIM:(j + 1) * HEAD_DIM] = qk_head(z[:, j * HEAD_DIM:(j + 1) * HEAD_DIM], gk_ref[...], 1.0)
    va_ref[...] = z[:, KV_WIDTH:].astype(BF16)

    half = GLA_V_WIDTH // 2
    for s in range(2):
        og = _dot(h, w_ref[:, _OFF_OG + s * half:_OFF_OG + (s + 1) * half])
        og_ref[:, s * half:(s + 1) * half] = (og * _sigmoid(og)).astype(BF16)
    qg_ref[...] = (_dot(h, w_ref[:, _OFF_QG:_OFF_QG + GLA_K_WIDTH]) * (GLA_DK ** -0.5)).astype(BF16)
    kg_ref[...] = _dot(h, w_ref[:, _OFF_KG:_OFF_KG + GLA_K_WIDTH]).astype(BF16)
    for s in range(2):
        vg_ref[:, s * half:(s + 1) * half] = _dot(h, w_ref[:, _OFF_VG + s * half:_OFF_VG + (s + 1) * half]).astype(BF16)


def _in_proj(x, g_pre, w_in, g_q, g_k, cos, sin, w_up, b_up, seq_len, tm):
    n_tok = x.shape[0]
    n_pos_blocks = seq_len // tm
    row = lambda w: pl.BlockSpec((tm, w), lambda i: (i, 0))
    whole = lambda a: pl.BlockSpec(a.shape, lambda i: (0,) * a.ndim)
    pos = pl.BlockSpec((tm, HEAD_DIM), lambda i: (i % n_pos_blocks, 0))
    out_widths = (ATTN_WIDTH, KV_WIDTH, KV_WIDTH, GLA_K_WIDTH, GLA_K_WIDTH, GLA_V_WIDTH, GLA_V_WIDTH)
    out_shape = [jax.ShapeDtypeStruct((n_tok, w), BF16) for w in out_widths]
    out_shape += [jax.ShapeDtypeStruct((n_tok, GLA_K_WIDTH), F32)] * 2
    out_specs = [row(w) for w in out_widths] + [row(GLA_K_WIDTH)] * 2
    return pl.pallas_call(
        _inproj_kernel,
        grid=(n_tok // tm,),
        in_specs=[row(D_MODEL), whole(g_pre), whole(w_in), whole(g_q), whole(g_k), pos, pos,
                  whole(w_up), whole(b_up)],
        out_specs=out_specs,
        out_shape=out_shape,
        compiler_params=pltpu.CompilerParams(dimension_semantics=("parallel",),
                                             vmem_limit_bytes=V7X_VMEM_LIMIT_BYTES),
        name="in_proj",
    )(x, g_pre, w_in, g_q, g_k, cos, sin, w_up, b_up)


_REDUCE_SLAB_ROWS = 32


def _col_reduce(x, reduce_fn):
    rows, n = x.shape
    slab = reduce_fn(x.reshape(rows // _REDUCE_SLAB_ROWS, _REDUCE_SLAB_ROWS, n), axis=0)
    return reduce_fn(slab, axis=0, keepdims=True)


def _attn_kernel(q_ref, k_ref, vt_ref, o_ref, st_even_ref, st_odd_ref, acc_ref):
    tq = q_ref.shape[0]
    n_chunks, _, kc = vt_ref.shape
    assert n_chunks % 2 == 0
    acc_ref[...] = jnp.zeros_like(acc_ref)

    def key_chunk(c):
        return k_ref[pl.ds(pl.multiple_of(c * kc, kc), kc), :]

    def scores(g, k, st_ref, m):
        st = _dot_nt(k, q_ref[:, g * HEAD_DIM:(g + 1) * HEAD_DIM])
        st_ref[g] = st
        return jnp.maximum(m, _col_reduce(st, jnp.max))

    def accumulate(g, vt, st_ref, m_old, m_new, l):
        alpha = jnp.exp2(m_old - m_new)
        p = jnp.exp2(st_ref[g] - m_new)
        acc_ref[g] = alpha * acc_ref[g] + _dot(vt, p.astype(BF16))
        return alpha * l + _col_reduce(p, jnp.sum)

    def step(c, st_ref, st_next_ref, m_old, m_cur, l_all):
        k_next, vt = key_chunk(c + 1), vt_ref[c]
        m_next, l_out = [], []
        for g in range(KV_GROUP):
            m_next.append(scores(g, k_next, st_next_ref, m_cur[g]))
            l_out.append(accumulate(g, vt, st_ref, m_old[g], m_cur[g], l_all[g]))
        return tuple(m_next), tuple(l_out)

    def pair(i, carry):
        m_old, m_cur, l_all = carry
        m_odd, l_all = step(2 * i, st_even_ref, st_odd_ref, m_old, m_cur, l_all)
        m_even, l_all = step(2 * i + 1, st_odd_ref, st_even_ref, m_cur, m_odd, l_all)
        return m_odd, m_even, l_all

    m_init = tuple(jnp.full((1, tq), -jnp.inf, F32) for _ in range(KV_GROUP))
    l_init = tuple(jnp.zeros((1, tq), F32) for _ in range(KV_GROUP))
    m_first = tuple(scores(g, key_chunk(0), st_even_ref, m_init[g]) for g in range(KV_GROUP))
    m_old, m_cur, l_all = lax.fori_loop(0, n_chunks // 2 - 1, pair, (m_init, m_first, l_init))
    m_last, l_all = step(n_chunks - 2, st_even_ref, st_odd_ref, m_old, m_cur, l_all)
    vt_last = vt_ref[n_chunks - 1]
    l_all = tuple(accumulate(g, vt_last, st_odd_ref, m_cur[g], m_last[g], l_all[g]) for g in range(KV_GROUP))
    for g in range(KV_GROUP):
        o_ref[:, g * HEAD_DIM:(g + 1) * HEAD_DIM] = (acc_ref[g] / l_all[g]).T.astype(o_ref.dtype)


def _attention(q, k, v, batch, seq_len, tq, kc):
    n_q = seq_len // tq
    n_chunks = seq_len // kc
    vt = v.reshape(batch, n_chunks, kc, N_KV_HEADS, HEAD_DIM).transpose(0, 3, 1, 4, 2)
    vt = vt.reshape(batch * N_KV_HEADS, n_chunks, HEAD_DIM, kc)
    q_spec = pl.BlockSpec((tq, KV_GROUP * HEAD_DIM), lambda b, h, i: (b * n_q + i, h))
    k_spec = pl.BlockSpec((seq_len, HEAD_DIM), lambda b, h, i: (b, h))
    vt_spec = pl.BlockSpec((None, n_chunks, HEAD_DIM, kc), lambda b, h, i: (b * N_KV_HEADS + h, 0, 0, 0))
    return pl.pallas_call(
        _attn_kernel,
        grid=(batch, N_KV_HEADS, n_q),
        in_specs=[q_spec, k_spec, vt_spec],
        out_specs=q_spec,
        out_shape=jax.ShapeDtypeStruct(q.shape, BF16),
        scratch_shapes=[pltpu.VMEM((KV_GROUP, kc, tq), F32), pltpu.VMEM((KV_GROUP, kc, tq), F32),
                        pltpu.VMEM((KV_GROUP, HEAD_DIM, tq), F32)],
        compiler_params=pltpu.CompilerParams(dimension_semantics=("parallel", "parallel", "parallel"),
                                             vmem_limit_bytes=V7X_VMEM_LIMIT_BYTES),
        name="attention",
    )(q, k, vt)


def _gla_kernel(qf_ref, kf_ref, vf_ref, laf_ref, qb_ref, kb_ref, vb_ref, lab_ref, of_ref, ob_ref, sf_ref, sb_ref):
    @pl.when(pl.program_id(2) == 0)
    def _():
        sf_ref[...] = jnp.zeros_like(sf_ref)
        sb_ref[...] = jnp.zeros_like(sb_ref)

    n = GLA_CHUNK
    tl = qf_ref.shape[0]
    n_chunks = tl // n
    row = lax.broadcasted_iota(jnp.int32, (tl, tl), 0)
    col = lax.broadcasted_iota(jnp.int32, (tl, tl), 1)
    same_chunk = (row // n) == (col // n)
    direction = {
        False: dict(cum=(same_chunk & (col <= row)).astype(BF16), keep=same_chunk & (col <= row), edge=n - 1,
                    refs=(qf_ref, kf_ref, vf_ref, laf_ref, of_ref, sf_ref)),
        True: dict(cum=(same_chunk & (col >= row)).astype(BF16), keep=same_chunk & (col > row), edge=0,
                   refs=(qb_ref, kb_ref, vb_ref, lab_ref, ob_ref, sb_ref)),
    }
    chains = [(reverse, hd) for reverse in (False, True) for hd in range(qf_ref.shape[1] // GLA_DK)]
    kcols = lambda hd: slice(hd * GLA_DK, (hd + 1) * GLA_DK)
    vcols = lambda hd: slice(hd * GLA_DV, (hd + 1) * GLA_DV)
    chunk_rows = lambda c: slice(c * n, (c + 1) * n)

    b_all = []
    for reverse, hd in chains:
        d = direction[reverse]
        la = d["refs"][3][:, kcols(hd)]
        la_hi = la.astype(BF16)
        la_lo = (la - la_hi.astype(F32)).astype(BF16)
        b_all.append(_dot(d["cum"], la_hi) + _dot(d["cum"], la_lo))

    q_dec_all, k_dec_all, k_out_all, b_edge_all = [], [], [], []
    for (reverse, hd), b in zip(chains, b_all):
        d = direction[reverse]
        edge = d["edge"]
        b_edge = [b[c * n + edge:c * n + edge + 1, :] for c in range(n_chunks)]
        b_edge_rows = jnp.concatenate([jnp.broadcast_to(e, (n, GLA_DK)) for e in b_edge], axis=0)
        q = d["refs"][0][:, kcols(hd)].astype(F32)
        k = d["refs"][1][:, kcols(hd)].astype(F32)
        q_dec_all.append((q * jnp.exp(b)).astype(BF16))
        k_dec_all.append((k * jnp.exp(-b)).astype(BF16))
        k_out_all.append((k * jnp.exp(b_edge_rows - b)).astype(BF16))
        b_edge_all.append(b_edge)

    a_all = [jnp.where(direction[reverse]["keep"], _dot_nt(q_dec, k_dec), 0.0).astype(BF16)
             for (reverse, _), q_dec, k_dec in zip(chains, q_dec_all, k_dec_all)]
    o_intra_all = [_dot(a, direction[reverse]["refs"][2][:, vcols(hd)]) for (reverse, hd), a in zip(chains, a_all)]
    kv_all = [[_dot_tn(direction[reverse]["refs"][2][chunk_rows(c), vcols(hd)], k_out[chunk_rows(c), :])
               for c in range(n_chunks)] for (reverse, hd), k_out in zip(chains, k_out_all)]

    states = [direction[reverse]["refs"][5][hd] for reverse, hd in chains]
    for t in range(n_chunks):
        for i, (reverse, hd) in enumerate(chains):
            c = n_chunks - 1 - t if reverse else t
            o_ref = direction[reverse]["refs"][4]
            o_ref[chunk_rows(c), vcols(hd)] = (o_intra_all[i][chunk_rows(c), :]
                                               + _dot_nt(q_dec_all[i][chunk_rows(c), :], states[i].astype(BF16)))
            states[i] = states[i] * jnp.exp(b_edge_all[i][c]) + kv_all[i][c]
    for i, (reverse, hd) in enumerate(chains):
        direction[reverse]["refs"][5][hd] = states[i]


def _gla(q, k, v, la_f, la_b, batch, seq_len, tl, heads_per_step):
    n_blk = seq_len // tl
    hp = heads_per_step
    fwd = lambda w: pl.BlockSpec((tl, hp * w), lambda b, h, i: (b * n_blk + i, h))
    bwd = lambda w: pl.BlockSpec((tl, hp * w), lambda b, h, i: (b * n_blk + n_blk - 1 - i, h))
    out = jax.ShapeDtypeStruct((q.shape[0], GLA_V_WIDTH), F32)
    state = pltpu.VMEM((hp, GLA_DV, GLA_DK), F32)
    return pl.pallas_call(
        _gla_kernel,
        grid=(batch, N_GLA_HEADS // hp, n_blk),
        in_specs=[fwd(GLA_DK), fwd(GLA_DK), fwd(GLA_DV), fwd(GLA_DK),
                  bwd(GLA_DK), bwd(GLA_DK), bwd(GLA_DV), bwd(GLA_DK)],
        out_specs=[fwd(GLA_DV), bwd(GLA_DV)],
        out_shape=[out, out],
        scratch_shapes=[state, state],
        compiler_params=pltpu.CompilerParams(dimension_semantics=("parallel", "parallel", "arbitrary"),
                                             vmem_limit_bytes=V7X_VMEM_LIMIT_BYTES),
        name="gla",
    )(q, k, v, la_f, q, k, v, la_b)


def _outproj_kernel(attn_ref, of_ref, ob_ref, og_ref, x_ref, w_ref, ggla_ref, gpost_ref, gffn_ref, y_ref, h_ref):
    half = x_ref.shape[0] // 2
    halves = (slice(0, half), slice(half, 2 * half))
    mixes = []
    for rows in halves:
        mix = _dot(attn_ref[rows, :], w_ref[:ATTN_WIDTH, :])
        for hd in range(N_GLA_HEADS):
            cols = slice(hd * GLA_DV, (hd + 1) * GLA_DV)
            o = _rms(of_ref[rows, cols] + ob_ref[rows, cols], ggla_ref[...]) * og_ref[rows, cols].astype(F32)
            mix += _dot(o.astype(BF16), w_ref[ATTN_WIDTH + hd * GLA_DV:ATTN_WIDTH + (hd + 1) * GLA_DV, :])
        mixes.append(mix)
    for rows, mix in zip(halves, mixes):
        y = x_ref[rows, :] + _rms(mix, gpost_ref[...])
        y_ref[rows, :] = y
        h_ref[rows, :] = _rms(y, gffn_ref[...]).astype(BF16)


def _out_proj(attn, o_f, o_b, og, x, w_out, g_gla, g_post, g_pre_ffn, tm):
    n_tok = x.shape[0]
    row = lambda w: pl.BlockSpec((tm, w), lambda i: (i, 0))
    whole = lambda a: pl.BlockSpec(a.shape, lambda i: (0,) * a.ndim)
    return pl.pallas_call(
        _outproj_kernel,
        grid=(n_tok // tm,),
        in_specs=[row(ATTN_WIDTH), row(GLA_V_WIDTH), row(GLA_V_WIDTH), row(GLA_V_WIDTH), row(D_MODEL),
                  whole(w_out), whole(g_gla), whole(g_post), whole(g_pre_ffn)],
        out_specs=[row(D_MODEL), row(D_MODEL)],
        out_shape=[jax.ShapeDtypeStruct(x.shape, F32), jax.ShapeDtypeStruct(x.shape, BF16)],
        compiler_params=pltpu.CompilerParams(dimension_semantics=("parallel",),
                                             vmem_limit_bytes=V7X_VMEM_LIMIT_BYTES),
        name="out_proj",
    )(attn, o_f, o_b, og, x, w_out, g_gla, g_post, g_pre_ffn)


def _ffn_kernel(h_ref, wg_ref, wu_ref, wd_ref, y_ref):
    @pl.when(pl.program_id(1) == 0)
    def _():
        y_ref[...] = jnp.zeros_like(y_ref)

    h = h_ref[...]
    gate = _dot(h, wg_ref[...])
    up = _dot(h, wu_ref[...])
    y_ref[...] += _dot((gate * _sigmoid(gate) * up).astype(BF16), wd_ref[...])


def _ffn(h, w_gate_up, w_down, tm, tf):
    n_tok = h.shape[0]
    n_f = D_FF // tf
    row = pl.BlockSpec((tm, D_MODEL), lambda i, f: (i, 0))
    return pl.pallas_call(
        _ffn_kernel,
        grid=(n_tok // tm, n_f),
        in_specs=[row,
                  pl.BlockSpec((D_MODEL, tf), lambda i, f: (0, f)),
                  pl.BlockSpec((D_MODEL, tf), lambda i, f: (0, f + n_f)),
                  pl.BlockSpec((tf, D_MODEL), lambda i, f: (f, 0))],
        out_specs=row,
        out_shape=jax.ShapeDtypeStruct(h.shape, F32),
        compiler_params=pltpu.CompilerParams(dimension_semantics=("parallel", "arbitrary"),
                                             vmem_limit_bytes=V7X_VMEM_LIMIT_BYTES),
        name="ffn",
    )(h, w_gate_up, w_gate_up, w_down)


def _ple_kernel(x_ref, ffn_ref, p_ref, gffn_ref, gpre_ref, wg_ref, wp_ref, gpost_ref, y_ref):
    emb = _dot(p_ref[...].astype(BF16), wp_ref[...])
    half = x_ref.shape[0] // 2
    halves = (slice(0, half), slice(half, 2 * half))
    xs = [x_ref[rows, :] + _rms(ffn_ref[rows, :], gffn_ref[...]) for rows in halves]
    gates = [_dot(_rms(x, gpre_ref[...]).astype(BF16), wg_ref[...]) for x in xs]
    for rows, x, gate in zip(halves, xs, gates):
        y_ref[rows, :] = x + _rms(emb[rows, :] * _sigmoid(gate), gpost_ref[...])


def _ple(x, ffn_out, p, g_post_ffn, g_pre, w_gate, w_proj, g_post, tm):
    n_tok = x.shape[0]
    row = lambda w: pl.BlockSpec((tm, w), lambda i: (i, 0))
    whole = lambda a: pl.BlockSpec(a.shape, lambda i: (0,) * a.ndim)
    return pl.pallas_call(
        _ple_kernel,
        grid=(n_tok // tm,),
        in_specs=[row(D_MODEL), row(D_MODEL), row(D_PLE), whole(g_post_ffn), whole(g_pre), whole(w_gate),
                  whole(w_proj), whole(g_post)],
        out_specs=row(D_MODEL),
        out_shape=jax.ShapeDtypeStruct(x.shape, F32),
        compiler_params=pltpu.CompilerParams(dimension_semantics=("parallel",),
                                             vmem_limit_bytes=V7X_VMEM_LIMIT_BYTES),
        name="ple",
    )(x, ffn_out, p, g_post_ffn, g_pre, w_gate, w_proj, g_post)


def _rope_tables(seq_len):
    n_rows = seq_len // GRID_W
    freqs = ROPE_THETA ** (-jnp.arange(0, ROPE_AXIS_DIM, 2, dtype=F32) / ROPE_AXIS_DIM)
    ang_r = jnp.arange(n_rows, dtype=F32)[:, None] * freqs[None, :]
    ang_c = jnp.arange(GRID_W, dtype=F32)[:, None] * freqs[None, :]
    zero_r, zero_c = jnp.zeros_like(ang_r), jnp.zeros_like(ang_c)

    def spread(row_part, col_part):
        return (row_part[:, None, :] + col_part[None, :, :]).reshape(seq_len, HEAD_DIM)

    cos = spread(jnp.concatenate([jnp.cos(ang_r), jnp.cos(ang_r), zero_r, zero_r], axis=-1),
                 jnp.concatenate([zero_c, zero_c, jnp.cos(ang_c), jnp.cos(ang_c)], axis=-1))
    sin = spread(jnp.concatenate([-jnp.sin(ang_r), jnp.sin(ang_r), zero_r, zero_r], axis=-1),
                 jnp.concatenate([zero_c, zero_c, -jnp.sin(ang_c), jnp.sin(ang_c)], axis=-1))
    return cos, sin


def _layer(x3, p3, w, *, tm_in, tq, kc, tl, gla_heads, tm_out, tm_ffn, tf, tm_ple):
    batch, seq_len, _ = x3.shape
    x = x3.reshape(batch * seq_len, D_MODEL)
    p = p3.reshape(batch * seq_len, D_PLE)
    cos, sin = _rope_tables(seq_len)
    qa, ka, va, qg, kg, vg, og, la_f, la_b = _in_proj(
        x, w["g_pre_mix"], w["w_in"], w["g_q"], w["g_k"], cos, sin, w["w_up"], w["b_up"],
        seq_len, tm_in)
    attn = _attention(qa, ka, va, batch, seq_len, tq, kc)
    o_f, o_b = _gla(qg, kg, vg, la_f, la_b, batch, seq_len, tl, gla_heads)
    x, h = _out_proj(attn, o_f, o_b, og, x, w["w_out"], w["g_gla_norm"], w["g_post_mix"], w["g_pre_ffn"], tm_out)
    ffn_out = _ffn(h, w["w_gate_up"], w["w_down"], tm_ffn, tf)
    x = _ple(x, ffn_out, p, w["g_post_ffn"], w["g_ple_pre"], w["w_ple_gate"], w["w_ple_proj"], w["g_ple_post"],
             tm_ple)
    return x.reshape(batch, seq_len, D_MODEL)


def _layer_weights(i, g_pre_mix, w_in, g_q, g_k, w_gf_up, b_gf, w_gb_up, b_gb, g_gla_norm, w_out, g_post_mix,
                   g_pre_ffn, w_gate_up, w_down, g_post_ffn, g_ple_pre, w_ple_gate, w_ple_proj, g_ple_post):
    vec = lambda g: g[i].astype(F32)[None, :]
    zeros = jnp.zeros((GATE_RANK, GLA_K_WIDTH), F32)
    w_up = jnp.concatenate([jnp.concatenate([w_gf_up[i], zeros], axis=1),
                            jnp.concatenate([zeros, w_gb_up[i]], axis=1)], axis=0)
    return {
        "g_pre_mix": vec(g_pre_mix), "w_in": w_in[i].astype(BF16), "g_q": vec(g_q), "g_k": vec(g_k),
        "w_up": w_up.astype(BF16), "b_up": jnp.concatenate([b_gf[i], b_gb[i]]).astype(F32)[None, :],
        "g_gla_norm": vec(g_gla_norm), "w_out": w_out[i].astype(BF16), "g_post_mix": vec(g_post_mix),
        "g_pre_ffn": vec(g_pre_ffn), "w_gate_up": w_gate_up[i].astype(BF16), "w_down": w_down[i].astype(BF16),
        "g_post_ffn": vec(g_post_ffn), "g_ple_pre": vec(g_ple_pre), "w_ple_gate": w_ple_gate[i].astype(BF16),
        "w_ple_proj": w_ple_proj[i].astype(BF16), "g_ple_post": vec(g_ple_post),
    }


_TILES = dict(tm_in=512, tq=1024, kc=512, tl=256, gla_heads=4, tm_out=512, tm_ffn=1024, tf=512, tm_ple=512)


def kernel(x_prompt, x_sample, p_prompt, p_sample, g_pre_mix, w_in, g_q, g_k, w_gf_up, b_gf, w_gb_up, b_gb,
           g_gla_norm, w_out, g_post_mix, g_pre_ffn, w_gate_up, w_down, g_post_ffn, g_ple_pre, w_ple_gate,
           w_ple_proj, g_ple_post):
    y_prompt = x_prompt
    y_sample = x_sample
    for i in range(g_pre_mix.shape[0]):
        w = _layer_weights(i, g_pre_mix, w_in, g_q, g_k, w_gf_up, b_gf, w_gb_up, b_gb, g_gla_norm, w_out,
                           g_post_mix, g_pre_ffn, w_gate_up, w_down, g_post_ffn, g_ple_pre, w_ple_gate,
                           w_ple_proj, g_ple_post)
        y_prompt = _layer(y_prompt, p_prompt[i], w, **_TILES)
        y_sample = _layer(y_sample, p_sample[i], w, **_TILES)
    return (y_prompt, y_sample)
```

```python
import functools

import jax
import jax.numpy as jnp
from jax import lax
from jax.experimental import pallas as pl
from jax.experimental.pallas import tpu as pltpu

D_MODEL = 2048
HEAD_DIM = 128
N_ATTN_HEADS = 8
N_KV_HEADS = 2
KV_GROUP = N_ATTN_HEADS // N_KV_HEADS
ATTN_WIDTH = N_ATTN_HEADS * HEAD_DIM
KV_WIDTH = N_KV_HEADS * HEAD_DIM
ROPE_THETA = 10000.0
ROPE_AXIS_DIM = HEAD_DIM // 2
GRID_W = 64
N_GLA_HEADS = 4
GLA_DK = 128
GLA_DV = 256
GLA_K_WIDTH = N_GLA_HEADS * GLA_DK
GLA_V_WIDTH = N_GLA_HEADS * GLA_DV
GATE_RANK = 16
GATE_TAU = 16.0
GLA_CHUNK = 64
MIX_WIDTH = ATTN_WIDTH + GLA_V_WIDTH
D_FF = 5632
D_PLE = 256
EPS = 1e-6

_OFF_QA = 0
_OFF_KA = _OFF_QA + ATTN_WIDTH
_OFF_VA = _OFF_KA + KV_WIDTH
_OFF_QG = _OFF_VA + KV_WIDTH
_OFF_KG = _OFF_QG + GLA_K_WIDTH
_OFF_VG = _OFF_KG + GLA_K_WIDTH
_OFF_OG = _OFF_VG + GLA_V_WIDTH
_OFF_LOW = _OFF_OG + GLA_V_WIDTH

V7X_VMEM_LIMIT_BYTES = 56 * 1024 * 1024

F32 = jnp.float32
BF16 = jnp.bfloat16

_Q_SCALE = HEAD_DIM ** -0.5 * 1.4426950408889634


def _rms(xf, g):
    ms = jnp.mean(xf * xf, axis=-1, keepdims=True)
    return xf * lax.rsqrt(ms + EPS) * g


def _dot(a, b):
    return jnp.dot(a, b, preferred_element_type=F32)


def _dot_nt(a, b):
    return lax.dot_general(a, b, (((1,), (1,)), ((), ())), preferred_element_type=F32)


def _dot_tn(a, b):
    return lax.dot_general(a, b, (((0,), (0,)), ((), ())), preferred_element_type=F32)


def _sigmoid(x):
    return 1.0 / (1.0 + jnp.exp(-x))


def _rope(z, cos, sin_signed, first_half):
    swapped = jnp.where(first_half, pltpu.roll(z, HEAD_DIM - 32, axis=1), pltpu.roll(z, 32, axis=1))
    return z * cos + swapped * sin_signed


def _inproj_kernel(x_ref, gpre_ref, w_ref, gq_ref, gk_ref, cos_ref, sin_ref, wup_ref, bup_ref,
                   qa_ref, ka_ref, va_ref, qg_ref, kg_ref, vg_ref, og_ref, laf_ref, lab_ref):
    h = _rms(x_ref[...], gpre_ref[...]).astype(BF16)
    cos = cos_ref[...]
    sin = sin_ref[...]
    lane = lax.broadcasted_iota(jnp.int32, cos.shape, 1)
    first_half = (lane % ROPE_AXIS_DIM) < (ROPE_AXIS_DIM // 2)

    def qk_head(z, g, scale):
        return (_rope(_rms(z, g), cos, sin, first_half) * scale).astype(BF16)


    low = _dot(h, w_ref[:, _OFF_LOW:]).astype(BF16)
    pre = _dot(low, wup_ref[...]) + bup_ref[...]
    log_a = (jnp.minimum(pre, 0.0) - jnp.log(1.0 + jnp.exp(-jnp.abs(pre)))) * (1.0 / GATE_TAU)
    laf_ref[...] = log_a[:, :GLA_K_WIDTH]
    lab_ref[...] = log_a[:, GLA_K_WIDTH:]

    slab = 4 * HEAD_DIM
    for s in range(ATTN_WIDTH // slab):
        z = _dot(h, w_ref[:, _OFF_QA + s * slab:_OFF_QA + (s + 1) * slab])
        for j in range(slab // HEAD_DIM):
            c = s * slab + j * HEAD_DIM
            qa_ref[:, c:c + HEAD_DIM] = qk_head(z[:, j * HEAD_DIM:(j + 1) * HEAD_DIM], gq_ref[...], _Q_SCALE)

    z = _dot(h, w_ref[:, _OFF_KA:_OFF_KA + 2 * KV_WIDTH])
    for j in range(N_KV_HEADS):
        ka_ref[:, j * HEAD_DIM:(j + 1) * HEAD_DIM] = qk_head(z[:, j * HEAD_DIM:(j + 1) * HEAD_DIM], gk_ref[...], 1.0)
    va_ref[...] = z[:, KV_WIDTH:].astype(BF16)

    half = GLA_V_WIDTH // 2
    for s in range(2):
        og = _dot(h, w_ref[:, _OFF_OG + s * half:_OFF_OG + (s + 1) * half])
        og_ref[:, s * half:(s + 1) * half] = (og * _sigmoid(og)).astype(BF16)
    qg_ref[...] = (_dot(h, w_ref[:, _OFF_QG:_OFF_QG + GLA_K_WIDTH]) * (GLA_DK ** -0.5)).astype(BF16)
    kg_ref[...] = _dot(h, w_ref[:, _OFF_KG:_OFF_KG + GLA_K_WIDTH]).astype(BF16)
    for s in range(2):
        vg_ref[:, s * half:(s + 1) * half] = _dot(h, w_ref[:, _OFF_VG + s * half:_OFF_VG + (s + 1) * half]).astype(BF16)


def _in_proj(x, g_pre, w_in, g_q, g_k, cos, sin, w_up, b_up, seq_len, tm):
    n_tok = x.shape[0]
    n_pos_blocks = seq_len // tm
    row = lambda w: pl.BlockSpec((tm, w), lambda i: (i, 0))
    whole = lambda a: pl.BlockSpec(a.shape, lambda i: (0,) * a.ndim)
    pos = pl.BlockSpec((tm, HEAD_DIM), lambda i: (i % n_pos_blocks, 0))
    out_widths = (ATTN_WIDTH, KV_WIDTH, KV_WIDTH, GLA_K_WIDTH, GLA_K_WIDTH, GLA_V_WIDTH, GLA_V_WIDTH)
    out_shape = [jax.ShapeDtypeStruct((n_tok, w), BF16) for w in out_widths]
    out_shape += [jax.ShapeDtypeStruct((n_tok, GLA_K_WIDTH), F32)] * 2
    out_specs = [row(w) for w in out_widths] + [row(GLA_K_WIDTH)] * 2
    return pl.pallas_call(
        _inproj_kernel,
        grid=(n_tok // tm,),
        in_specs=[row(D_MODEL), whole(g_pre), whole(w_in), whole(g_q), whole(g_k), pos, pos,
                  whole(w_up), whole(b_up)],
        out_specs=out_specs,
        out_shape=out_shape,
        compiler_params=pltpu.CompilerParams(dimension_semantics=("parallel",),
                                             vmem_limit_bytes=V7X_VMEM_LIMIT_BYTES),
        name="in_proj",
    )(x, g_pre, w_in, g_q, g_k, cos, sin, w_up, b_up)


_REDUCE_SLAB_ROWS = 32
_BF16_TILE_ROWS = 16


def _col_reduce(x, reduce_fn):
    rows, n = x.shape
    slab = reduce_fn(x.reshape(rows // _REDUCE_SLAB_ROWS, _REDUCE_SLAB_ROWS, n), axis=0)
    return reduce_fn(slab, axis=0, keepdims=True)


def _attn_kernel(q_ref, k_ref, vt_ref, o_ref, st_even_ref, st_odd_ref, acc_ref, *, mxu_denominator):
    tq = q_ref.shape[0]
    n_chunks, _, kc = vt_ref.shape
    assert n_chunks % 2 == 0
    acc_ref[...] = jnp.zeros_like(acc_ref)

    def key_chunk(c):
        return k_ref[pl.ds(pl.multiple_of(c * kc, kc), kc), :]

    def scores(g, k, st_ref, m):
        st = _dot_nt(k, q_ref[:, g * HEAD_DIM:(g + 1) * HEAD_DIM])
        st_ref[g] = st
        return jnp.maximum(m, _col_reduce(st, jnp.max))

    def accumulate(g, vt, st_ref, m_old, m_new, l):
        alpha = jnp.exp2(m_old - m_new)
        p = jnp.exp2(st_ref[g] - m_new)
        acc_ref[g] = alpha * acc_ref[g] + _dot(vt, p.astype(BF16))
        if mxu_denominator:
            return l
        return alpha * l + _col_reduce(p, jnp.sum)

    def step(c, st_ref, st_next_ref, m_old, m_cur, l_all):
        k_next, vt = key_chunk(c + 1), vt_ref[c]
        m_next = tuple(scores(g, k_next, st_next_ref, m_cur[g]) for g in range(KV_GROUP))
        l_out = tuple(accumulate(g, vt, st_ref, m_old[g], m_cur[g], l_all[g]) for g in range(KV_GROUP))
        return m_next, l_out

    def pair(i, carry):
        m_old, m_cur, l_all = carry
        m_odd, l_all = step(2 * i, st_even_ref, st_odd_ref, m_old, m_cur, l_all)
        m_even, l_all = step(2 * i + 1, st_odd_ref, st_even_ref, m_cur, m_odd, l_all)
        return m_odd, m_even, l_all

    m_init = tuple(jnp.full((1, tq), -jnp.inf, F32) for _ in range(KV_GROUP))
    l_init = tuple(jnp.zeros((1, tq), F32) for _ in range(KV_GROUP))
    m_first = tuple(scores(g, key_chunk(0), st_even_ref, m_init[g]) for g in range(KV_GROUP))
    m_old, m_cur, l_all = lax.fori_loop(0, n_chunks // 2 - 1, pair, (m_init, m_first, l_init))
    m_last, l_all = step(n_chunks - 2, st_even_ref, st_odd_ref, m_old, m_cur, l_all)
    vt_last = vt_ref[n_chunks - 1]
    l_all = tuple(accumulate(g, vt_last, st_odd_ref, m_cur[g], m_last[g], l_all[g]) for g in range(KV_GROUP))
    for g in range(KV_GROUP):
        denominator = acc_ref[g, HEAD_DIM:HEAD_DIM + 1, :] if mxu_denominator else l_all[g]
        o_ref[:, g * HEAD_DIM:(g + 1) * HEAD_DIM] = (acc_ref[g, :HEAD_DIM, :] / denominator).T.astype(o_ref.dtype)


def _attention(q, k, v, batch, seq_len, tq, kc, mxu_denominator):
    n_q = seq_len // tq
    n_chunks = seq_len // kc
    vt = v.reshape(batch, n_chunks, kc, N_KV_HEADS, HEAD_DIM).transpose(0, 3, 1, 4, 2)
    vt = vt.reshape(batch * N_KV_HEADS, n_chunks, HEAD_DIM, kc)
    vt_rows = HEAD_DIM
    if mxu_denominator:
        vt_rows += _BF16_TILE_ROWS
        extra = jnp.zeros(vt.shape[:2] + (_BF16_TILE_ROWS, kc), BF16).at[:, :, 0, :].set(1.0)
        vt = jnp.concatenate([vt, extra], axis=2)
    q_spec = pl.BlockSpec((tq, KV_GROUP * HEAD_DIM), lambda b, h, i: (b * n_q + i, h))
    k_spec = pl.BlockSpec((seq_len, HEAD_DIM), lambda b, h, i: (b, h))
    vt_spec = pl.BlockSpec((None, n_chunks, vt_rows, kc), lambda b, h, i: (b * N_KV_HEADS + h, 0, 0, 0))
    return pl.pallas_call(
        functools.partial(_attn_kernel, mxu_denominator=mxu_denominator),
        grid=(batch, N_KV_HEADS, n_q),
        in_specs=[q_spec, k_spec, vt_spec],
        out_specs=q_spec,
        out_shape=jax.ShapeDtypeStruct(q.shape, BF16),
        scratch_shapes=[pltpu.VMEM((KV_GROUP, kc, tq), F32), pltpu.VMEM((KV_GROUP, kc, tq), F32),
                        pltpu.VMEM((KV_GROUP, vt_rows, tq), F32)],
        compiler_params=pltpu.CompilerParams(dimension_semantics=("parallel", "parallel", "parallel"),
                                             vmem_limit_bytes=V7X_VMEM_LIMIT_BYTES),
        name="attention",
    )(q, k, vt)


def _gla_kernel(qf_ref, kf_ref, vf_ref, laf_ref, qb_ref, kb_ref, vb_ref, lab_ref, of_ref, ob_ref, sf_ref, sb_ref):
    @pl.when(pl.program_id(2) == 0)
    def _():
        sf_ref[...] = jnp.zeros_like(sf_ref)
        sb_ref[...] = jnp.zeros_like(sb_ref)

    n = GLA_CHUNK
    tl = qf_ref.shape[0]
    n_chunks = tl // n
    row = lax.broadcasted_iota(jnp.int32, (tl, tl), 0)
    col = lax.broadcasted_iota(jnp.int32, (tl, tl), 1)
    same_chunk = (row // n) == (col // n)
    direction = {
        False: dict(cum=(same_chunk & (col <= row)).astype(BF16), keep=same_chunk & (col <= row), edge=n - 1,
                    refs=(qf_ref, kf_ref, vf_ref, laf_ref, of_ref, sf_ref)),
        True: dict(cum=(same_chunk & (col >= row)).astype(BF16), keep=same_chunk & (col > row), edge=0,
                   refs=(qb_ref, kb_ref, vb_ref, lab_ref, ob_ref, sb_ref)),
    }
    chains = [(reverse, hd) for reverse in (False, True) for hd in range(qf_ref.shape[1] // GLA_DK)]
    kcols = lambda hd: slice(hd * GLA_DK, (hd + 1) * GLA_DK)
    vcols = lambda hd: slice(hd * GLA_DV, (hd + 1) * GLA_DV)
    chunk_rows = lambda c: slice(c * n, (c + 1) * n)

    b_all = []
    for reverse, hd in chains:
        d = direction[reverse]
        la = d["refs"][3][:, kcols(hd)]
        la_hi = la.astype(BF16)
        la_lo = (la - la_hi.astype(F32)).astype(BF16)
        b_all.append(_dot(d["cum"], la_hi) + _dot(d["cum"], la_lo))

    q_dec_all, k_dec_all, k_out_all, b_edge_all = [], [], [], []
    for (reverse, hd), b in zip(chains, b_all):
        d = direction[reverse]
        edge = d["edge"]
        b_edge = [b[c * n + edge:c * n + edge + 1, :] for c in range(n_chunks)]
        b_edge_rows = jnp.concatenate([jnp.broadcast_to(e, (n, GLA_DK)) for e in b_edge], axis=0)
        q = d["refs"][0][:, kcols(hd)].astype(F32)
        k = d["refs"][1][:, kcols(hd)].astype(F32)
        q_dec_all.append((q * jnp.exp(b)).astype(BF16))
        k_dec_all.append((k * jnp.exp(-b)).astype(BF16))
        k_out_all.append((k * jnp.exp(b_edge_rows - b)).astype(BF16))
        b_edge_all.append(b_edge)

    a_all = [jnp.where(direction[reverse]["keep"], _dot_nt(q_dec, k_dec), 0.0).astype(BF16)
             for (reverse, _), q_dec, k_dec in zip(chains, q_dec_all, k_dec_all)]
    o_intra_all = [_dot(a, direction[reverse]["refs"][2][:, vcols(hd)]) for (reverse, hd), a in zip(chains, a_all)]
    kv_all = [[_dot_tn(direction[reverse]["refs"][2][chunk_rows(c), vcols(hd)], k_out[chunk_rows(c), :])
               for c in range(n_chunks)] for (reverse, hd), k_out in zip(chains, k_out_all)]

    states = [direction[reverse]["refs"][5][hd] for reverse, hd in chains]
    for t in range(n_chunks):
        for i, (reverse, hd) in enumerate(chains):
            c = n_chunks - 1 - t if reverse else t
            o_ref = direction[reverse]["refs"][4]
            o_ref[chunk_rows(c), vcols(hd)] = (o_intra_all[i][chunk_rows(c), :]
                                               + _dot_nt(q_dec_all[i][chunk_rows(c), :], states[i].astype(BF16)))
            states[i] = states[i] * jnp.exp(b_edge_all[i][c]) + kv_all[i][c]
    for i, (reverse, hd) in enumerate(chains):
        direction[reverse]["refs"][5][hd] = states[i]


def _gla(q, k, v, la_f, la_b, batch, seq_len, tl, heads_per_step):
    n_blk = seq_len // tl
    hp = heads_per_step
    fwd = lambda w: pl.BlockSpec((tl, hp * w), lambda b, h, i: (b * n_blk + i, h))
    bwd = lambda w: pl.BlockSpec((tl, hp * w), lambda b, h, i: (b * n_blk + n_blk - 1 - i, h))
    out = jax.ShapeDtypeStruct((q.shape[0], GLA_V_WIDTH), F32)
    state = pltpu.VMEM((hp, GLA_DV, GLA_DK), F32)
    return pl.pallas_call(
        _gla_kernel,
        grid=(batch, N_GLA_HEADS // hp, n_blk),
        in_specs=[fwd(GLA_DK), fwd(GLA_DK), fwd(GLA_DV), fwd(GLA_DK),
                  bwd(GLA_DK), bwd(GLA_DK), bwd(GLA_DV), bwd(GLA_DK)],
        out_specs=[fwd(GLA_DV), bwd(GLA_DV)],
        out_shape=[out, out],
        scratch_shapes=[state, state],
        compiler_params=pltpu.CompilerParams(dimension_semantics=("parallel", "parallel", "arbitrary"),
                                             vmem_limit_bytes=V7X_VMEM_LIMIT_BYTES),
        name="gla",
    )(q, k, v, la_f, q, k, v, la_b)


def _outproj_kernel(attn_ref, of_ref, ob_ref, og_ref, x_ref, w_ref, ggla_ref, gpost_ref, gffn_ref, y_ref, h_ref):
    half = x_ref.shape[0] // 2
    halves = (slice(0, half), slice(half, 2 * half))
    mixes = []
    for rows in halves:
        mix = _dot(attn_ref[rows, :], w_ref[:ATTN_WIDTH, :])
        for hd in range(N_GLA_HEADS):
            cols = slice(hd * GLA_DV, (hd + 1) * GLA_DV)
            o = _rms(of_ref[rows, cols] + ob_ref[rows, cols], ggla_ref[...]) * og_ref[rows, cols].astype(F32)
            mix += _dot(o.astype(BF16), w_ref[ATTN_WIDTH + hd * GLA_DV:ATTN_WIDTH + (hd + 1) * GLA_DV, :])
        mixes.append(mix)
    for rows, mix in zip(halves, mixes):
        y = x_ref[rows, :] + _rms(mix, gpost_ref[...])
        y_ref[rows, :] = y
        h_ref[rows, :] = _rms(y, gffn_ref[...]).astype(BF16)


def _out_proj(attn, o_f, o_b, og, x, w_out, g_gla, g_post, g_pre_ffn, tm):
    n_tok = x.shape[0]
    row = lambda w: pl.BlockSpec((tm, w), lambda i: (i, 0))
    whole = lambda a: pl.BlockSpec(a.shape, lambda i: (0,) * a.ndim)
    return pl.pallas_call(
        _outproj_kernel,
        grid=(n_tok // tm,),
        in_specs=[row(ATTN_WIDTH), row(GLA_V_WIDTH), row(GLA_V_WIDTH), row(GLA_V_WIDTH), row(D_MODEL),
                  whole(w_out), whole(g_gla), whole(g_post), whole(g_pre_ffn)],
        out_specs=[row(D_MODEL), row(D_MODEL)],
        out_shape=[jax.ShapeDtypeStruct(x.shape, F32), jax.ShapeDtypeStruct(x.shape, BF16)],
        compiler_params=pltpu.CompilerParams(dimension_semantics=("parallel",),
                                             vmem_limit_bytes=V7X_VMEM_LIMIT_BYTES),
        name="out_proj",
    )(attn, o_f, o_b, og, x, w_out, g_gla, g_post, g_pre_ffn)


def _ffn_kernel(h_ref, wg_ref, wu_ref, wd_ref, y_ref):
    @pl.when(pl.program_id(1) == 0)
    def _():
        y_ref[...] = jnp.zeros_like(y_ref)

    h = h_ref[...]
    gate = _dot(h, wg_ref[...])
    up = _dot(h, wu_ref[...])
    y_ref[...] += _dot((gate * _sigmoid(gate) * up).astype(BF16), wd_ref[...])


def _ffn(h, w_gate_up, w_down, tm, tf):
    n_tok = h.shape[0]
    n_f = D_FF // tf
    row = pl.BlockSpec((tm, D_MODEL), lambda i, f: (i, 0))
    return pl.pallas_call(
        _ffn_kernel,
        grid=(n_tok // tm, n_f),
        in_specs=[row,
                  pl.BlockSpec((D_MODEL, tf), lambda i, f: (0, f)),
                  pl.BlockSpec((D_MODEL, tf), lambda i, f: (0, f + n_f)),
                  pl.BlockSpec((tf, D_MODEL), lambda i, f: (f, 0))],
        out_specs=row,
        out_shape=jax.ShapeDtypeStruct(h.shape, F32),
        compiler_params=pltpu.CompilerParams(dimension_semantics=("parallel", "arbitrary"),
                                             vmem_limit_bytes=V7X_VMEM_LIMIT_BYTES),
        name="ffn",
    )(h, w_gate_up, w_gate_up, w_down)


def _ple_kernel(x_ref, ffn_ref, p_ref, gffn_ref, gpre_ref, wg_ref, wp_ref, gpost_ref, y_ref):
    emb = _dot(p_ref[...].astype(BF16), wp_ref[...])
    half = x_ref.shape[0] // 2
    halves = (slice(0, half), slice(half, 2 * half))
    xs = [x_ref[rows, :] + _rms(ffn_ref[rows, :], gffn_ref[...]) for rows in halves]
    gates = [_dot(_rms(x, gpre_ref[...]).astype(BF16), wg_ref[...]) for x in xs]
    for rows, x, gate in zip(halves, xs, gates):
        y_ref[rows, :] = x + _rms(emb[rows, :] * _sigmoid(gate), gpost_ref[...])


def _ple(x, ffn_out, p, g_post_ffn, g_pre, w_gate, w_proj, g_post, tm):
    n_tok = x.shape[0]
    row = lambda w: pl.BlockSpec((tm, w), lambda i: (i, 0))
    whole = lambda a: pl.BlockSpec(a.shape, lambda i: (0,) * a.ndim)
    return pl.pallas_call(
        _ple_kernel,
        grid=(n_tok // tm,),
        in_specs=[row(D_MODEL), row(D_MODEL), row(D_PLE), whole(g_post_ffn), whole(g_pre), whole(w_gate),
                  whole(w_proj), whole(g_post)],
        out_specs=row(D_MODEL),
        out_shape=jax.ShapeDtypeStruct(x.shape, F32),
        compiler_params=pltpu.CompilerParams(dimension_semantics=("parallel",),
                                             vmem_limit_bytes=V7X_VMEM_LIMIT_BYTES),
        name="ple",
    )(x, ffn_out, p, g_post_ffn, g_pre, w_gate, w_proj, g_post)


def _rope_tables(seq_len):
    n_rows = seq_len // GRID_W
    freqs = ROPE_THETA ** (-jnp.arange(0, ROPE_AXIS_DIM, 2, dtype=F32) / ROPE_AXIS_DIM)
    ang_r = jnp.arange(n_rows, dtype=F32)[:, None] * freqs[None, :]
    ang_c = jnp.arange(GRID_W, dtype=F32)[:, None] * freqs[None, :]
    zero_r, zero_c = jnp.zeros_like(ang_r), jnp.zeros_like(ang_c)

    def spread(row_part, col_part):
        return (row_part[:, None, :] + col_part[None, :, :]).reshape(seq_len, HEAD_DIM)

    cos = spread(jnp.concatenate([jnp.cos(ang_r), jnp.cos(ang_r), zero_r, zero_r], axis=-1),
                 jnp.concatenate([zero_c, zero_c, jnp.cos(ang_c), jnp.cos(ang_c)], axis=-1))
    sin = spread(jnp.concatenate([-jnp.sin(ang_r), jnp.sin(ang_r), zero_r, zero_r], axis=-1),
                 jnp.concatenate([zero_c, zero_c, -jnp.sin(ang_c), jnp.sin(ang_c)], axis=-1))
    return cos, sin


def _layer(x3, p3, w, *, tm_in, tq, kc, attn_mxu_denominator, tl, gla_heads, tm_out, tm_ffn, tf, tm_ple):
    batch, seq_len, _ = x3.shape
    x = x3.reshape(batch * seq_len, D_MODEL)
    p = p3.reshape(batch * seq_len, D_PLE)
    cos, sin = _rope_tables(seq_len)
    qa, ka, va, qg, kg, vg, og, la_f, la_b = _in_proj(
        x, w["g_pre_mix"], w["w_in"], w["g_q"], w["g_k"], cos, sin, w["w_up"], w["b_up"],
        seq_len, tm_in)
    attn = _attention(qa, ka, va, batch, seq_len, tq, kc, attn_mxu_denominator)
    o_f, o_b = _gla(qg, kg, vg, la_f, la_b, batch, seq_len, tl, gla_heads)
    x, h = _out_proj(attn, o_f, o_b, og, x, w["w_out"], w["g_gla_norm"], w["g_post_mix"], w["g_pre_ffn"], tm_out)
    ffn_out = _ffn(h, w["w_gate_up"], w["w_down"], tm_ffn, tf)
    x = _ple(x, ffn_out, p, w["g_post_ffn"], w["g_ple_pre"], w["w_ple_gate"], w["w_ple_proj"], w["g_ple_post"],
             tm_ple)
    return x.reshape(batch, seq_len, D_MODEL)


def _layer_weights(i, g_pre_mix, w_in, g_q, g_k, w_gf_up, b_gf, w_gb_up, b_gb, g_gla_norm, w_out, g_post_mix,
                   g_pre_ffn, w_gate_up, w_down, g_post_ffn, g_ple_pre, w_ple_gate, w_ple_proj, g_ple_post):
    vec = lambda g: g[i].astype(F32)[None, :]
    zeros = jnp.zeros((GATE_RANK, GLA_K_WIDTH), F32)
    w_up = jnp.concatenate([jnp.concatenate([w_gf_up[i], zeros], axis=1),
                            jnp.concatenate([zeros, w_gb_up[i]], axis=1)], axis=0)
    return {
        "g_pre_mix": vec(g_pre_mix), "w_in": w_in[i].astype(BF16), "g_q": vec(g_q), "g_k": vec(g_k),
        "w_up": w_up.astype(BF16), "b_up": jnp.concatenate([b_gf[i], b_gb[i]]).astype(F32)[None, :],
        "g_gla_norm": vec(g_gla_norm), "w_out": w_out[i].astype(BF16), "g_post_mix": vec(g_post_mix),
        "g_pre_ffn": vec(g_pre_ffn), "w_gate_up": w_gate_up[i].astype(BF16), "w_down": w_down[i].astype(BF16),
        "g_post_ffn": vec(g_post_ffn), "g_ple_pre": vec(g_ple_pre), "w_ple_gate": w_ple_gate[i].astype(BF16),
        "w_ple_proj": w_ple_proj[i].astype(BF16), "g_ple_post": vec(g_ple_post),
    }


_TILES = dict(tm_in=512, tq=1024, kc=512, tl=256, gla_heads=4, tm_out=512, tm_ffn=1024, tf=512, tm_ple=512)


def kernel(x_prompt, x_sample, p_prompt, p_sample, g_pre_mix, w_in, g_q, g_k, w_gf_up, b_gf, w_gb_up, b_gb,
           g_gla_norm, w_out, g_post_mix, g_pre_ffn, w_gate_up, w_down, g_post_ffn, g_ple_pre, w_ple_gate,
           w_ple_proj, g_ple_post):
    y_prompt = x_prompt
    y_sample = x_sample
    for i in range(g_pre_mix.shape[0]):
        w = _layer_weights(i, g_pre_mix, w_in, g_q, g_k, w_gf_up, b_gf, w_gb_up, b_gb, g_gla_norm, w_out,
                           g_post_mix, g_pre_ffn, w_gate_up, w_down, g_post_ffn, g_ple_pre, w_ple_gate,
                           w_ple_proj, g_ple_post)
        y_prompt = _layer(y_prompt, p_prompt[i], w, attn_mxu_denominator=True, **_TILES)
        y_sample = _layer(y_sample, p_sample[i], w, attn_mxu_denominator=False, **_TILES)
    return (y_prompt, y_sample)
```

```python
import jax
import jax.numpy as jnp
from jax import lax
from jax.experimental import pallas as pl
from jax.experimental.pallas import tpu as pltpu

D_MODEL = 2048
HEAD_DIM = 128
N_ATTN_HEADS = 8
N_KV_HEADS = 2
KV_GROUP = N_ATTN_HEADS // N_KV_HEADS
ATTN_WIDTH = N_ATTN_HEADS * HEAD_DIM
KV_WIDTH = N_KV_HEADS * HEAD_DIM
ROPE_THETA = 10000.0
ROPE_AXIS_DIM = HEAD_DIM // 2
GRID_W = 64
N_GLA_HEADS = 4
GLA_DK = 128
GLA_DV = 256
GLA_K_WIDTH = N_GLA_HEADS * GLA_DK
GLA_V_WIDTH = N_GLA_HEADS * GLA_DV
GATE_RANK = 16
GATE_TAU = 16.0
GLA_CHUNK = 64
MIX_WIDTH = ATTN_WIDTH + GLA_V_WIDTH
D_FF = 5632
D_PLE = 256
EPS = 1e-6

_OFF_QA = 0
_OFF_KA = _OFF_QA + ATTN_WIDTH
_OFF_VA = _OFF_KA + KV_WIDTH
_OFF_QG = _OFF_VA + KV_WIDTH
_OFF_KG = _OFF_QG + GLA_K_WIDTH
_OFF_VG = _OFF_KG + GLA_K_WIDTH
_OFF_OG = _OFF_VG + GLA_V_WIDTH
_OFF_LOW = _OFF_OG + GLA_V_WIDTH

V7X_VMEM_LIMIT_BYTES = 56 * 1024 * 1024

F32 = jnp.float32
BF16 = jnp.bfloat16

_Q_SCALE = HEAD_DIM ** -0.5 * 1.4426950408889634


def _rms(xf, g):
    ms = jnp.mean(xf * xf, axis=-1, keepdims=True)
    return xf * lax.rsqrt(ms + EPS) * g


def _dot(a, b):
    return jnp.dot(a, b, preferred_element_type=F32)


def _dot_nt(a, b):
    return lax.dot_general(a, b, (((1,), (1,)), ((), ())), preferred_element_type=F32)


def _dot_tn(a, b):
    return lax.dot_general(a, b, (((0,), (0,)), ((), ())), preferred_element_type=F32)


def _sigmoid(x):
    return 1.0 / (1.0 + jnp.exp(-x))


def _rope(z, cos, sin_signed, first_half):
    swapped = jnp.where(first_half, pltpu.roll(z, HEAD_DIM - 32, axis=1), pltpu.roll(z, 32, axis=1))
    return z * cos + swapped * sin_signed


def _inproj_kernel(x_ref, gpre_ref, w_ref, gq_ref, gk_ref, cos_ref, sin_ref, wup_ref, bup_ref,
                   qa_ref, ka_ref, va_ref, qg_ref, kg_ref, vg_ref, og_ref, laf_ref, lab_ref):
    h = _rms(x_ref[...], gpre_ref[...]).astype(BF16)
    cos = cos_ref[...]
    sin = sin_ref[...]
    lane = lax.broadcasted_iota(jnp.int32, cos.shape, 1)
    first_half = (lane % ROPE_AXIS_DIM) < (ROPE_AXIS_DIM // 2)

    def qk_head(z, g, scale):
        return (_rope(_rms(z, g), cos, sin, first_half) * scale).astype(BF16)


    low = _dot(h, w_ref[:, _OFF_LOW:]).astype(BF16)
    pre = _dot(low, wup_ref[...]) + bup_ref[...]
    log_a = (jnp.minimum(pre, 0.0) - jnp.log(1.0 + jnp.exp(-jnp.abs(pre)))) * (1.0 / GATE_TAU)
    laf_ref[...] = log_a[:, :GLA_K_WIDTH]
    lab_ref[...] = log_a[:, GLA_K_WIDTH:]

    slab = 4 * HEAD_DIM
    for s in range(ATTN_WIDTH // slab):
        z = _dot(h, w_ref[:, _OFF_QA + s * slab:_OFF_QA + (s + 1) * slab])
        for j in range(slab // HEAD_DIM):
            c = s * slab + j * HEAD_DIM
            qa_ref[:, c:c + HEAD_DIM] = qk_head(z[:, j * HEAD_DIM:(j + 1) * HEAD_DIM], gq_ref[...], _Q_SCALE)

    z = _dot(h, w_ref[:, _OFF_KA:_OFF_KA + 2 * KV_WIDTH])
    for j in range(N_KV_HEADS):
        ka_ref[:, j * HEAD_DIM:(j + 1) * HEAD_DIM] = qk_head(z[:, j * HEAD_DIM:(j + 1) * HEAD_DIM], gk_ref[...], 1.0)
    va_ref[...] = z[:, KV_WIDTH:].astype(BF16)

    half = GLA_V_WIDTH // 2
    for s in range(2):
        og = _dot(h, w_ref[:, _OFF_OG + s * half:_OFF_OG + (s + 1) * half])
        og_ref[:, s * half:(s + 1) * half] = (og * _sigmoid(og)).astype(BF16)
    qg_ref[...] = (_dot(h, w_ref[:, _OFF_QG:_OFF_QG + GLA_K_WIDTH]) * (GLA_DK ** -0.5)).astype(BF16)
    kg_ref[...] = _dot(h, w_ref[:, _OFF_KG:_OFF_KG + GLA_K_WIDTH]).astype(BF16)
    for s in range(2):
        vg_ref[:, s * half:(s + 1) * half] = _dot(h, w_ref[:, _OFF_VG + s * half:_OFF_VG + (s + 1) * half]).astype(BF16)


def _in_proj(x, g_pre, w_in, g_q, g_k, cos, sin, w_up, b_up, seq_len, tm):
    n_tok = x.shape[0]
    n_pos_blocks = seq_len // tm
    row = lambda w: pl.BlockSpec((tm, w), lambda i: (i, 0))
    whole = lambda a: pl.BlockSpec(a.shape, lambda i: (0,) * a.ndim)
    pos = pl.BlockSpec((tm, HEAD_DIM), lambda i: (i % n_pos_blocks, 0))
    out_widths = (ATTN_WIDTH, KV_WIDTH, KV_WIDTH, GLA_K_WIDTH, GLA_K_WIDTH, GLA_V_WIDTH, GLA_V_WIDTH)
    out_shape = [jax.ShapeDtypeStruct((n_tok, w), BF16) for w in out_widths]
    out_shape += [jax.ShapeDtypeStruct((n_tok, GLA_K_WIDTH), F32)] * 2
    out_specs = [row(w) for w in out_widths] + [row(GLA_K_WIDTH)] * 2
    return pl.pallas_call(
        _inproj_kernel,
        grid=(n_tok // tm,),
        in_specs=[row(D_MODEL), whole(g_pre), whole(w_in), whole(g_q), whole(g_k), pos, pos,
                  whole(w_up), whole(b_up)],
        out_specs=out_specs,
        out_shape=out_shape,
        compiler_params=pltpu.CompilerParams(dimension_semantics=("parallel",),
                                             vmem_limit_bytes=V7X_VMEM_LIMIT_BYTES),
        name="in_proj",
    )(x, g_pre, w_in, g_q, g_k, cos, sin, w_up, b_up)


_REDUCE_SLAB_ROWS = 32


def _col_reduce(x, reduce_fn):
    rows, n = x.shape
    slab = reduce_fn(x.reshape(rows // _REDUCE_SLAB_ROWS, _REDUCE_SLAB_ROWS, n), axis=0)
    return reduce_fn(slab, axis=0, keepdims=True)


def _attn_kernel(q_ref, k_ref, vt_ref, o_ref, st_even_ref, st_odd_ref, acc_ref):
    tq = q_ref.shape[0]
    n_chunks, _, kc = vt_ref.shape
    assert n_chunks % 2 == 0
    acc_ref[...] = jnp.zeros_like(acc_ref)

    def key_chunk(c):
        return k_ref[pl.ds(pl.multiple_of(c * kc, kc), kc), :]

    def scores(g, k, st_ref, m):
        st = _dot_nt(k, q_ref[:, g * HEAD_DIM:(g + 1) * HEAD_DIM])
        st_ref[g] = st
        return jnp.maximum(m, _col_reduce(st, jnp.max))

    def accumulate(g, vt, st_ref, m_old, m_new, l):
        alpha = jnp.exp2(m_old - m_new)
        p = jnp.exp2(st_ref[g] - m_new)
        acc_ref[g] = alpha * acc_ref[g] + _dot(vt, p.astype(BF16))
        return alpha * l + _col_reduce(p, jnp.sum)

    def step(c, st_ref, st_next_ref, m_old, m_cur, l_all):
        k_next, vt = key_chunk(c + 1), vt_ref[c]
        m_next = tuple(scores(g, k_next, st_next_ref, m_cur[g]) for g in range(KV_GROUP))
        l_out = tuple(accumulate(g, vt, st_ref, m_old[g], m_cur[g], l_all[g]) for g in range(KV_GROUP))
        return m_next, l_out

    def pair(i, carry):
        m_old, m_cur, l_all = carry
        m_odd, l_all = step(2 * i, st_even_ref, st_odd_ref, m_old, m_cur, l_all)
        m_even, l_all = step(2 * i + 1, st_odd_ref, st_even_ref, m_cur, m_odd, l_all)
        return m_odd, m_even, l_all

    m_init = tuple(jnp.full((1, tq), -jnp.inf, F32) for _ in range(KV_GROUP))
    l_init = tuple(jnp.zeros((1, tq), F32) for _ in range(KV_GROUP))
    m_first = tuple(scores(g, key_chunk(0), st_even_ref, m_init[g]) for g in range(KV_GROUP))
    m_old, m_cur, l_all = lax.fori_loop(0, n_chunks // 2 - 1, pair, (m_init, m_first, l_init))
    m_last, l_all = step(n_chunks - 2, st_even_ref, st_odd_ref, m_old, m_cur, l_all)
    vt_last = vt_ref[n_chunks - 1]
    l_all = tuple(accumulate(g, vt_last, st_odd_ref, m_cur[g], m_last[g], l_all[g]) for g in range(KV_GROUP))
    for g in range(KV_GROUP):
        o_ref[:, g * HEAD_DIM:(g + 1) * HEAD_DIM] = (acc_ref[g] / l_all[g]).T.astype(o_ref.dtype)


def _attention(q, k, v, batch, seq_len, tq, kc):
    n_q = seq_len // tq
    n_chunks = seq_len // kc
    vt = v.reshape(batch, n_chunks, kc, N_KV_HEADS, HEAD_DIM).transpose(0, 3, 1, 4, 2)
    vt = vt.reshape(batch * N_KV_HEADS, n_chunks, HEAD_DIM, kc)
    q_spec = pl.BlockSpec((tq, KV_GROUP * HEAD_DIM), lambda b, h, i: (b * n_q + i, h))
    k_spec = pl.BlockSpec((seq_len, HEAD_DIM), lambda b, h, i: (b, h))
    vt_spec = pl.BlockSpec((None, n_chunks, HEAD_DIM, kc), lambda b, h, i: (b * N_KV_HEADS + h, 0, 0, 0))
    return pl.pallas_call(
        _attn_kernel,
        grid=(batch, N_KV_HEADS, n_q),
        in_specs=[q_spec, k_spec, vt_spec],
        out_specs=q_spec,
        out_shape=jax.ShapeDtypeStruct(q.shape, BF16),
        scratch_shapes=[pltpu.VMEM((KV_GROUP, kc, tq), F32), pltpu.VMEM((KV_GROUP, kc, tq), F32),
                        pltpu.VMEM((KV_GROUP, HEAD_DIM, tq), F32)],
        compiler_params=pltpu.CompilerParams(dimension_semantics=("parallel", "parallel", "parallel"),
                                             vmem_limit_bytes=V7X_VMEM_LIMIT_BYTES),
        name="attention",
    )(q, k, vt)


def _gla_kernel(qf_ref, kf_ref, vf_ref, laf_ref, qb_ref, kb_ref, vb_ref, lab_ref, of_ref, ob_ref, sf_ref, sb_ref):
    @pl.when(pl.program_id(2) == 0)
    def _():
        sf_ref[...] = jnp.zeros_like(sf_ref)
        sb_ref[...] = jnp.zeros_like(sb_ref)

    n = GLA_CHUNK
    tl = qf_ref.shape[0]
    n_chunks = tl // n
    row = lax.broadcasted_iota(jnp.int32, (tl, tl), 0)
    col = lax.broadcasted_iota(jnp.int32, (tl, tl), 1)
    same_chunk = (row // n) == (col // n)
    direction = {
        False: dict(cum=(same_chunk & (col <= row)).astype(BF16), keep=same_chunk & (col <= row), edge=n - 1,
                    refs=(qf_ref, kf_ref, vf_ref, laf_ref, of_ref, sf_ref)),
        True: dict(cum=(same_chunk & (col >= row)).astype(BF16), keep=same_chunk & (col > row), edge=0,
                   refs=(qb_ref, kb_ref, vb_ref, lab_ref, ob_ref, sb_ref)),
    }
    chains = [(reverse, hd) for reverse in (False, True) for hd in range(qf_ref.shape[1] // GLA_DK)]
    kcols = lambda hd: slice(hd * GLA_DK, (hd + 1) * GLA_DK)
    vcols = lambda hd: slice(hd * GLA_DV, (hd + 1) * GLA_DV)
    chunk_rows = lambda c: slice(c * n, (c + 1) * n)

    b_all = []
    for reverse, hd in chains:
        d = direction[reverse]
        la = d["refs"][3][:, kcols(hd)]
        la_hi = la.astype(BF16)
        la_lo = (la - la_hi.astype(F32)).astype(BF16)
        b_all.append(_dot(d["cum"], la_hi) + _dot(d["cum"], la_lo))

    q_dec_all, k_dec_all, k_out_all, b_edge_all = [], [], [], []
    for (reverse, hd), b in zip(chains, b_all):
        d = direction[reverse]
        edge = d["edge"]
        b_edge = [b[c * n + edge:c * n + edge + 1, :] for c in range(n_chunks)]
        b_edge_rows = jnp.concatenate([jnp.broadcast_to(e, (n, GLA_DK)) for e in b_edge], axis=0)
        q = d["refs"][0][:, kcols(hd)].astype(F32)
        k = d["refs"][1][:, kcols(hd)].astype(F32)
        q_dec_all.append((q * jnp.exp(b)).astype(BF16))
        k_dec_all.append((k * jnp.exp(-b)).astype(BF16))
        k_out_all.append((k * jnp.exp(b_edge_rows - b)).astype(BF16))
        b_edge_all.append(b_edge)

    a_all = [jnp.where(direction[reverse]["keep"], _dot_nt(q_dec, k_dec), 0.0).astype(BF16)
             for (reverse, _), q_dec, k_dec in zip(chains, q_dec_all, k_dec_all)]
    o_intra_all = [_dot(a, direction[reverse]["refs"][2][:, vcols(hd)]) for (reverse, hd), a in zip(chains, a_all)]
    kv_all = [[_dot_tn(direction[reverse]["refs"][2][chunk_rows(c), vcols(hd)], k_out[chunk_rows(c), :])
               for c in range(n_chunks)] for (reverse, hd), k_out in zip(chains, k_out_all)]

    states = [direction[reverse]["refs"][5][hd] for reverse, hd in chains]
    for t in range(n_chunks):
        for i, (reverse, hd) in enumerate(chains):
            c = n_chunks - 1 - t if reverse else t
            o_ref = direction[reverse]["refs"][4]
            o_ref[chunk_rows(c), vcols(hd)] = (o_intra_all[i][chunk_rows(c), :]
                                               + _dot_nt(q_dec_all[i][chunk_rows(c), :], states[i].astype(BF16)))
            states[i] = states[i] * jnp.exp(b_edge_all[i][c]) + kv_all[i][c]
    for i, (reverse, hd) in enumerate(chains):
        direction[reverse]["refs"][5][hd] = states[i]


def _gla(q, k, v, la_f, la_b, batch, seq_len, tl, heads_per_step):
    n_blk = seq_len // tl
    hp = heads_per_step
    fwd = lambda w: pl.BlockSpec((tl, hp * w), lambda b, h, i: (b * n_blk + i, h))
    bwd = lambda w: pl.BlockSpec((tl, hp * w), lambda b, h, i: (b * n_blk + n_blk - 1 - i, h))
    out = jax.ShapeDtypeStruct((q.shape[0], GLA_V_WIDTH), F32)
    state = pltpu.VMEM((hp, GLA_DV, GLA_DK), F32)
    return pl.pallas_call(
        _gla_kernel,
        grid=(batch, N_GLA_HEADS // hp, n_blk),
        in_specs=[fwd(GLA_DK), fwd(GLA_DK), fwd(GLA_DV), fwd(GLA_DK),
                  bwd(GLA_DK), bwd(GLA_DK), bwd(GLA_DV), bwd(GLA_DK)],
        out_specs=[fwd(GLA_DV), bwd(GLA_DV)],
        out_shape=[out, out],
        scratch_shapes=[state, state],
        compiler_params=pltpu.CompilerParams(dimension_semantics=("parallel", "parallel", "arbitrary"),
                                             vmem_limit_bytes=V7X_VMEM_LIMIT_BYTES),
        name="gla",
    )(q, k, v, la_f, q, k, v, la_b)


def _outproj_kernel(attn_ref, of_ref, ob_ref, og_ref, x_ref, w_ref, ggla_ref, gpost_ref, gffn_ref, y_ref, h_ref):
    half = x_ref.shape[0] // 2
    halves = (slice(0, half), slice(half, 2 * half))
    mixes = []
    for rows in halves:
        mix = _dot(attn_ref[rows, :], w_ref[:ATTN_WIDTH, :])
        for hd in range(N_GLA_HEADS):
            cols = slice(hd * GLA_DV, (hd + 1) * GLA_DV)
            o = _rms(of_ref[rows, cols] + ob_ref[rows, cols], ggla_ref[...]) * og_ref[rows, cols].astype(F32)
            mix += _dot(o.astype(BF16), w_ref[ATTN_WIDTH + hd * GLA_DV:ATTN_WIDTH + (hd + 1) * GLA_DV, :])
        mixes.append(mix)
    for rows, mix in zip(halves, mixes):
        y = x_ref[rows, :] + _rms(mix, gpost_ref[...])
        y_ref[rows, :] = y
        h_ref[rows, :] = _rms(y, gffn_ref[...]).astype(BF16)


def _out_proj(attn, o_f, o_b, og, x, w_out, g_gla, g_post, g_pre_ffn, tm):
    n_tok = x.shape[0]
    row = lambda w: pl.BlockSpec((tm, w), lambda i: (i, 0))
    whole = lambda a: pl.BlockSpec(a.shape, lambda i: (0,) * a.ndim)
    return pl.pallas_call(
        _outproj_kernel,
        grid=(n_tok // tm,),
        in_specs=[row(ATTN_WIDTH), row(GLA_V_WIDTH), row(GLA_V_WIDTH), row(GLA_V_WIDTH), row(D_MODEL),
                  whole(w_out), whole(g_gla), whole(g_post), whole(g_pre_ffn)],
        out_specs=[row(D_MODEL), row(D_MODEL)],
        out_shape=[jax.ShapeDtypeStruct(x.shape, F32), jax.ShapeDtypeStruct(x.shape, BF16)],
        compiler_params=pltpu.CompilerParams(dimension_semantics=("parallel",),
                                             vmem_limit_bytes=V7X_VMEM_LIMIT_BYTES),
        name="out_proj",
    )(attn, o_f, o_b, og, x, w_out, g_gla, g_post, g_pre_ffn)


def _ffn_kernel(h_ref, wg_ref, wu_ref, wd_ref, y_ref):
    @pl.when(pl.program_id(1) == 0)
    def _():
        y_ref[...] = jnp.zeros_like(y_ref)

    h = h_ref[...]
    gate = _dot(h, wg_ref[...])
    up = _dot(h, wu_ref[...])
    y_ref[...] += _dot((gate * _sigmoid(gate) * up).astype(BF16), wd_ref[...])


def _ffn(h, w_gate_up, w_down, tm, tf):
    n_tok = h.shape[0]
    n_f = D_FF // tf
    row = pl.BlockSpec((tm, D_MODEL), lambda i, f: (i, 0))
    return pl.pallas_call(
        _ffn_kernel,
        grid=(n_tok // tm, n_f),
        in_specs=[row,
                  pl.BlockSpec((D_MODEL, tf), lambda i, f: (0, f)),
                  pl.BlockSpec((D_MODEL, tf), lambda i, f: (0, f + n_f)),
                  pl.BlockSpec((tf, D_MODEL), lambda i, f: (f, 0))],
        out_specs=row,
        out_shape=jax.ShapeDtypeStruct(h.shape, F32),
        compiler_params=pltpu.CompilerParams(dimension_semantics=("parallel", "arbitrary"),
                                             vmem_limit_bytes=V7X_VMEM_LIMIT_BYTES),
        name="ffn",
    )(h, w_gate_up, w_gate_up, w_down)


def _ple_kernel(x_ref, ffn_ref, p_ref, gffn_ref, gpre_ref, wg_ref, wp_ref, gpost_ref, y_ref):
    emb = _dot(p_ref[...].astype(BF16), wp_ref[...])
    half = x_ref.shape[0] // 2
    halves = (slice(0, half), slice(half, 2 * half))
    xs = [x_ref[rows, :] + _rms(ffn_ref[rows, :], gffn_ref[...]) for rows in halves]
    gates = [_dot(_rms(x, gpre_ref[...]).astype(BF16), wg_ref[...]) for x in xs]
    for rows, x, gate in zip(halves, xs, gates):
        y_ref[rows, :] = x + _rms(emb[rows, :] * _sigmoid(gate), gpost_ref[...])


def _ple(x, ffn_out, p, g_post_ffn, g_pre, w_gate, w_proj, g_post, tm):
    n_tok = x.shape[0]
    row = lambda w: pl.BlockSpec((tm, w), lambda i: (i, 0))
    whole = lambda a: pl.BlockSpec(a.shape, lambda i: (0,) * a.ndim)
    return pl.pallas_call(
        _ple_kernel,
        grid=(n_tok // tm,),
        in_specs=[row(D_MODEL), row(D_MODEL), row(D_PLE), whole(g_post_ffn), whole(g_pre), whole(w_gate),
                  whole(w_proj), whole(g_post)],
        out_specs=row(D_MODEL),
        out_shape=jax.ShapeDtypeStruct(x.shape, F32),
        compiler_params=pltpu.CompilerParams(dimension_semantics=("parallel",),
                                             vmem_limit_bytes=V7X_VMEM_LIMIT_BYTES),
        name="ple",
    )(x, ffn_out, p, g_post_ffn, g_pre, w_gate, w_proj, g_post)


def _rope_tables(seq_len):
    n_rows = seq_len // GRID_W
    freqs = ROPE_THETA ** (-jnp.arange(0, ROPE_AXIS_DIM, 2, dtype=F32) / ROPE_AXIS_DIM)
    ang_r = jnp.arange(n_rows, dtype=F32)[:, None] * freqs[None, :]
    ang_c = jnp.arange(GRID_W, dtype=F32)[:, None] * freqs[None, :]
    zero_r, zero_c = jnp.zeros_like(ang_r), jnp.zeros_like(ang_c)

    def spread(row_part, col_part):
        return (row_part[:, None, :] + col_part[None, :, :]).reshape(seq_len, HEAD_DIM)

    cos = spread(jnp.concatenate([jnp.cos(ang_r), jnp.cos(ang_r), zero_r, zero_r], axis=-1),
                 jnp.concatenate([zero_c, zero_c, jnp.cos(ang_c), jnp.cos(ang_c)], axis=-1))
    sin = spread(jnp.concatenate([-jnp.sin(ang_r), jnp.sin(ang_r), zero_r, zero_r], axis=-1),
                 jnp.concatenate([zero_c, zero_c, -jnp.sin(ang_c), jnp.sin(ang_c)], axis=-1))
    return cos, sin


def _layer(x3, p3, w, *, tm_in, tq, kc, tl, gla_heads, tm_out, tm_ffn, tf, tm_ple):
    batch, seq_len, _ = x3.shape
    x = x3.reshape(batch * seq_len, D_MODEL)
    p = p3.reshape(batch * seq_len, D_PLE)
    cos, sin = _rope_tables(seq_len)
    qa, ka, va, qg, kg, vg, og, la_f, la_b = _in_proj(
        x, w["g_pre_mix"], w["w_in"], w["g_q"], w["g_k"], cos, sin, w["w_up"], w["b_up"],
        seq_len, tm_in)
    attn = _attention(qa, ka, va, batch, seq_len, tq, kc)
    o_f, o_b = _gla(qg, kg, vg, la_f, la_b, batch, seq_len, tl, gla_heads)
    x, h = _out_proj(attn, o_f, o_b, og, x, w["w_out"], w["g_gla_norm"], w["g_post_mix"], w["g_pre_ffn"], tm_out)
    ffn_out = _ffn(h, w["w_gate_up"], w["w_down"], tm_ffn, tf)
    x = _ple(x, ffn_out, p, w["g_post_ffn"], w["g_ple_pre"], w["w_ple_gate"], w["w_ple_proj"], w["g_ple_post"],
             tm_ple)
    return x.reshape(batch, seq_len, D_MODEL)


def _layer_weights(i, g_pre_mix, w_in, g_q, g_k, w_gf_up, b_gf, w_gb_up, b_gb, g_gla_norm, w_out, g_post_mix,
                   g_pre_ffn, w_gate_up, w_down, g_post_ffn, g_ple_pre, w_ple_gate, w_ple_proj, g_ple_post):
    vec = lambda g: g[i].astype(F32)[None, :]
    zeros = jnp.zeros((GATE_RANK, GLA_K_WIDTH), F32)
    w_up = jnp.concatenate([jnp.concatenate([w_gf_up[i], zeros], axis=1),
                            jnp.concatenate([zeros, w_gb_up[i]], axis=1)], axis=0)
    return {
        "g_pre_mix": vec(g_pre_mix), "w_in": w_in[i].astype(BF16), "g_q": vec(g_q), "g_k": vec(g_k),
        "w_up": w_up.astype(BF16), "b_up": jnp.concatenate([b_gf[i], b_gb[i]]).astype(F32)[None, :],
        "g_gla_norm": vec(g_gla_norm), "w_out": w_out[i].astype(BF16), "g_post_mix": vec(g_post_mix),
        "g_pre_ffn": vec(g_pre_ffn), "w_gate_up": w_gate_up[i].astype(BF16), "w_down": w_down[i].astype(BF16),
        "g_post_ffn": vec(g_post_ffn), "g_ple_pre": vec(g_ple_pre), "w_ple_gate": w_ple_gate[i].astype(BF16),
        "w_ple_proj": w_ple_proj[i].astype(BF16), "g_ple_post": vec(g_ple_post),
    }


_TILES = dict(tm_in=512, tq=1024, kc=512, tl=256, gla_heads=4, tm_out=512, tm_ffn=1024, tf=512, tm_ple=512)


def kernel(x_prompt, x_sample, p_prompt, p_sample, g_pre_mix, w_in, g_q, g_k, w_gf_up, b_gf, w_gb_up, b_gb,
           g_gla_norm, w_out, g_post_mix, g_pre_ffn, w_gate_up, w_down, g_post_ffn, g_ple_pre, w_ple_gate,
           w_ple_proj, g_ple_post):
    y_prompt = x_prompt
    y_sample = x_sample
    for i in range(g_pre_mix.shape[0]):
        w = _layer_weights(i, g_pre_mix, w_in, g_q, g_k, w_gf_up, b_gf, w_gb_up, b_gb, g_gla_norm, w_out,
                           g_post_mix, g_pre_ffn, w_gate_up, w_down, g_post_ffn, g_ple_pre, w_ple_gate,
                           w_ple_proj, g_ple_post)
        y_prompt = _layer(y_prompt, p_prompt[i], w, **_TILES)
        y_sample = _layer(y_sample, p_sample[i], w, **_TILES)
    return (y_prompt, y_sample)
```
